```python
import jax, jax.numpy as jnp
from jax import lax
import numpy as np

D_MODEL = 2048
BATCH = 1
SEQ = 8192
DEPTH = 2

GRID_W = 64
CTX_LEN = 256

MLA_HEADS = 8
Q_LORA = 512
KV_LORA = 512
QK_NOPE = 128
QK_ROPE = 64
V_HEAD = 128
ROPE_THETA = 10000.0
Q_BLOCK = 128
ATTN_SCALE = (QK_NOPE + QK_ROPE) ** -0.5
ATTN_W = MLA_HEADS * V_HEAD

N_FGROUPS = 8
FGROUP_W = 128
FOURIER_W = N_FGROUPS * FGROUP_W

EVEN_IN_W = Q_LORA + KV_LORA + QK_ROPE + FOURIER_W

CONV_W = D_MODEL
CONV_K = 3
ODD_IN_W = 3 * CONV_W

N_EXPERTS = 32
N_GROUPS = 8
EXPERTS_PER_GROUP = N_EXPERTS // N_GROUPS
TOP_K = 2
GROUP_SCORE_K = 2
D_EXPERT = 512

N_MOD = 6
EPS = 1e-6
N_EVEN = (DEPTH + 1) // 2
N_ODD = DEPTH // 2

kernel_name = "hybrid_mla_fourier_shortconv_grouped_moe_dit"


def rmsnorm(x, g):
    xf = x.astype(jnp.float32)
    y = xf * lax.rsqrt(jnp.mean(xf * xf, axis=-1, keepdims=True) + EPS)
    return y.astype(x.dtype) * g


def ada_params(cvec, w, b, n=N_MOD):
    mod = jax.nn.silu(cvec) @ w[:, :n * D_MODEL] + b[:n * D_MODEL]
    mod = mod[..., None, :]
    return jnp.split(mod, n, axis=-1)


def modulate(x, g, shift, scale):
    return rmsnorm(x, g) * (1.0 + scale) + shift


def axial_rope_tables(n_tokens):
    rows_n = n_tokens // GRID_W
    row = jnp.repeat(jnp.arange(rows_n), GRID_W).astype(jnp.float32)
    col = jnp.tile(jnp.arange(GRID_W), rows_n).astype(jnp.float32)
    half = QK_ROPE // 2
    inv = ROPE_THETA ** (-jnp.arange(0, half, 2, dtype=jnp.float32) / half)
    ang_r = row[:, None] * inv
    ang_c = col[:, None] * inv
    shp = (1, n_tokens, 1, half // 2)
    return (jnp.cos(ang_r).reshape(shp), jnp.sin(ang_r).reshape(shp),
            jnp.cos(ang_c).reshape(shp), jnp.sin(ang_c).reshape(shp))


def rotate_half(x, cos, sin):
    x1, x2 = jnp.split(x, 2, axis=-1)
    return jnp.concatenate([x1 * cos - x2 * sin, x2 * cos + x1 * sin], axis=-1)


def axial_rope(x, tabs):
    cos_r, sin_r, cos_c, sin_c = [t.astype(x.dtype) for t in tabs]
    half = QK_ROPE // 2
    return jnp.concatenate([rotate_half(x[..., :half], cos_r, sin_r),
                            rotate_half(x[..., half:], cos_c, sin_c)], axis=-1)


def mla_queries(q_c, q_norm_g, q_up_w):
    q = rmsnorm(q_c, q_norm_g) @ q_up_w
    q = q.reshape(q.shape[:-1] + (MLA_HEADS, QK_NOPE + QK_ROPE))
    return q[..., :QK_NOPE], q[..., QK_NOPE:]


def mla_keys_values(kv_c, kv_norm_g, kv_up_w):
    kv = rmsnorm(kv_c, kv_norm_g) @ kv_up_w
    kv = kv.reshape(kv.shape[:-1] + (MLA_HEADS, QK_NOPE + V_HEAD))
    return kv[..., :QK_NOPE], kv[..., QK_NOPE:]


def attend(qn, qr, kn, kr, v):
    s = jnp.einsum('bqhd,bkhd->bhqk', qn, kn) + jnp.einsum('bqhr,bkr->bhqk', qr, kr)
    p = jax.nn.softmax(s.astype(jnp.float32) * ATTN_SCALE, axis=-1).astype(v.dtype)
    return jnp.einsum('bhqk,bkhd->bqhd', p, v)


def block_attention(qn, qr, kn, kr, v):
    b, s, h, _ = qn.shape
    nb = s // Q_BLOCK
    qn_b = jnp.moveaxis(qn.reshape(b, nb, Q_BLOCK, h, QK_NOPE), 1, 0)
    qr_b = jnp.moveaxis(qr.reshape(b, nb, Q_BLOCK, h, QK_ROPE), 1, 0)
    o = lax.map(lambda blk: attend(blk[0], blk[1], kn, kr, v), (qn_b, qr_b))
    return jnp.moveaxis(o, 0, 1).reshape(b, s, h * V_HEAD)


def fourier_mix(f):
    b, t, _ = f.shape
    fg = f.astype(jnp.float32).reshape(b, t, N_FGROUPS, FGROUP_W)
    out = jnp.fft.fft2(fg, axes=(1, 3), norm="ortho").real
    return out.reshape(b, t, FOURIER_W).astype(f.dtype)


def mla_fourier_mixer(h_lat, h_ctx, w_in, q_norm_g, q_up_w, kv_norm_g, kv_up_w, w_out,
                      rope_tabs, update_ctx):
    b, s, _ = h_lat.shape
    c0, c1, c2 = Q_LORA, Q_LORA + KV_LORA, Q_LORA + KV_LORA + QK_ROPE
    u = h_lat @ w_in
    qn, qr = mla_queries(u[..., :c0], q_norm_g, q_up_w)
    qr = axial_rope(qr, rope_tabs)
    kn, v = mla_keys_values(u[..., c0:c1], kv_norm_g, kv_up_w)
    kr = axial_rope(u[..., c1:c2][:, :, None, :], rope_tabs)[:, :, 0, :]
    uc_kv = h_ctx @ w_in[:, c0:c2]
    ckn, cv = mla_keys_values(uc_kv[..., :KV_LORA], kv_norm_g, kv_up_w)
    ckr = uc_kv[..., KV_LORA:]
    attn = block_attention(qn, qr,
                           jnp.concatenate([kn, ckn], axis=1),
                           jnp.concatenate([kr, ckr], axis=1),
                           jnp.concatenate([v, cv], axis=1))
    y_lat = jnp.concatenate([attn, fourier_mix(u[..., c2:])], axis=-1) @ w_out
    if not update_ctx:
        return y_lat, None
    cqn, cqr = mla_queries(h_ctx @ w_in[:, :c0], q_norm_g, q_up_w)
    c_attn = attend(cqn, cqr, ckn, ckr, cv).reshape(h_ctx.shape[0], h_ctx.shape[1], ATTN_W)
    c_four = fourier_mix(h_ctx @ w_in[:, c2:])
    y_ctx = jnp.concatenate([c_attn, c_four], axis=-1) @ w_out
    return y_lat, y_ctx


def depthwise_conv_centred(y, w):
    t = y.shape[1]
    pad = CONV_K // 2
    yp = jnp.pad(y, ((0, 0), (pad, pad), (0, 0)))
    return sum(yp[:, k:k + t] * w[k] for k in range(CONV_K))


def short_conv_mixer(h, w_in, conv_w, w_out):
    b_gate, c_gate, z = jnp.split(h @ w_in, 3, axis=-1)
    y = depthwise_conv_centred(c_gate * z, conv_w)
    return (b_gate * y) @ w_out


def grouped_moe(h, router_w, router_b, w_gate, w_up, w_down):
    scores = jax.nn.sigmoid((h @ router_w).astype(jnp.float32))
    biased = scores + router_b.astype(jnp.float32)
    grouped = biased.reshape(biased.shape[:-1] + (N_GROUPS, EXPERTS_PER_GROUP))
    group_score = lax.top_k(grouped, GROUP_SCORE_K)[0].sum(axis=-1)
    best_group = jnp.argmax(group_score, axis=-1)
    gmask = jnp.arange(N_GROUPS) == best_group[..., None]
    masked = jnp.where(gmask[..., None], grouped, -jnp.inf).reshape(biased.shape)
    _, idx = lax.top_k(masked, TOP_K)
    wts = jnp.take_along_axis(scores, idx, axis=-1)
    wts = wts / jnp.sum(wts, axis=-1, keepdims=True)
    gates = jnp.sum(jax.nn.one_hot(idx, N_EXPERTS, dtype=jnp.float32) * wts[..., None],
                    axis=-2).astype(h.dtype)
    y = jnp.zeros_like(h)
    for e in range(N_EXPERTS):
        a = jax.nn.silu(h @ w_gate[e]) * (h @ w_up[e])
        y = y + gates[..., e:e + 1] * (a @ w_down[e])
    return y


def setup_inputs(seed: int = 0) -> dict:
    key = jax.random.key(seed)
    ks = jax.random.split(key, 24)
    nrm = lambda k, shape, s: jax.random.normal(k, shape, jnp.float32) * s
    return {
        "x": nrm(ks[0], (BATCH, SEQ, D_MODEL), 1.0),
        "c": nrm(ks[1], (BATCH, D_MODEL), 1.0),
        "ctx": nrm(ks[2], (BATCH, CTX_LEN, D_MODEL), 1.0),
        "c_ctx": nrm(ks[3], (D_MODEL,), 1.0),
        "ada_w": nrm(ks[4], (DEPTH, D_MODEL, N_MOD * D_MODEL), 0.5 * D_MODEL ** -0.5),
        "ada_b": nrm(ks[5], (DEPTH, N_MOD * D_MODEL), 0.02),
        "norm1_g": 1.0 + nrm(ks[6], (DEPTH, D_MODEL), 0.02),
        "norm2_g": 1.0 + nrm(ks[7], (DEPTH, D_MODEL), 0.02),
        "attn_in_w": nrm(ks[8], (N_EVEN, D_MODEL, EVEN_IN_W), D_MODEL ** -0.5),
        "q_norm_g": 1.0 + nrm(ks[9], (N_EVEN, Q_LORA), 0.02),
        "q_up_w": nrm(ks[10], (N_EVEN, Q_LORA, MLA_HEADS * (QK_NOPE + QK_ROPE)), Q_LORA ** -0.5),
        "kv_norm_g": 1.0 + nrm(ks[11], (N_EVEN, KV_LORA), 0.02),
        "kv_up_w": nrm(ks[12], (N_EVEN, KV_LORA, MLA_HEADS * (QK_NOPE + V_HEAD)), KV_LORA ** -0.5),
        "conv_in_w": nrm(ks[13], (N_ODD, D_MODEL, ODD_IN_W), D_MODEL ** -0.5),
        "conv_w": nrm(ks[14], (N_ODD, CONV_K, CONV_W), CONV_K ** -0.5),
        "mix_out_w": nrm(ks[15], (DEPTH, D_MODEL, D_MODEL), D_MODEL ** -0.5),
        "router_w": nrm(ks[16], (D_MODEL, N_EXPERTS), D_MODEL ** -0.5),
        "router_b": nrm(ks[17], (N_EXPERTS,), 0.01),
        "moe_w_gate": nrm(ks[18], (DEPTH, N_EXPERTS, D_MODEL, D_EXPERT), D_MODEL ** -0.5),
        "moe_w_up": nrm(ks[19], (DEPTH, N_EXPERTS, D_MODEL, D_EXPERT), D_MODEL ** -0.5),
        "moe_w_down": nrm(ks[20], (DEPTH, N_EXPERTS, D_EXPERT, D_MODEL), D_EXPERT ** -0.5),
        "final_norm_g": 1.0 + nrm(ks[21], (D_MODEL,), 0.02),
    }


def reference(x, c, ctx, c_ctx, ada_w, ada_b, norm1_g, norm2_g, attn_in_w, q_norm_g, q_up_w,
              kv_norm_g, kv_up_w, conv_in_w, conv_w, mix_out_w, router_w, router_b,
              moe_w_gate, moe_w_up, moe_w_down, final_norm_g):
    rope_tabs = axial_rope_tables(x.shape[1])
    for i in range(DEPTH):
        j = i // 2
        update_ctx = any(k % 2 == 0 for k in range(i + 1, DEPTH))
        sh1, sc1, g1, sh2, sc2, g2 = ada_params(c, ada_w[i], ada_b[i])
        h = modulate(x, norm1_g[i], sh1, sc1)
        if update_ctx:
            csh1, csc1, cg1, csh2, csc2, cg2 = ada_params(c_ctx, ada_w[i], ada_b[i])
        elif i % 2 == 0:
            csh1, csc1 = ada_params(c_ctx, ada_w[i], ada_b[i], n=2)
        if i % 2 == 0:
            hc = modulate(ctx, norm1_g[i], csh1, csc1)
            y, yc = mla_fourier_mixer(h, hc, attn_in_w[j], q_norm_g[j], q_up_w[j], kv_norm_g[j],
                                      kv_up_w[j], mix_out_w[i], rope_tabs, update_ctx)
        else:
            y = short_conv_mixer(h, conv_in_w[j], conv_w[j], mix_out_w[i])
            if update_ctx:
                yc = short_conv_mixer(modulate(ctx, norm1_g[i], csh1, csc1),
                                      conv_in_w[j], conv_w[j], mix_out_w[i])
        x = x + g1 * y
        x = x + g2 * grouped_moe(modulate(x, norm2_g[i], sh2, sc2), router_w, router_b,
                                 moe_w_gate[i], moe_w_up[i], moe_w_down[i])
        if update_ctx:
            ctx = ctx + cg1 * yc
            ctx = ctx + cg2 * grouped_moe(modulate(ctx, norm2_g[i], csh2, csc2), router_w,
                                          router_b, moe_w_gate[i], moe_w_up[i], moe_w_down[i])
    return rmsnorm(x, final_norm_g)
```

```python
import functools

import numpy as np
import jax
import jax.numpy as jnp
from jax import lax
from jax.experimental import pallas as pl
from jax.experimental.pallas import tpu as pltpu

F32 = jnp.float32
BF16 = jnp.bfloat16

D = 2048
S = 8192
CTX = 256
TT = S + CTX
GRID_W = 64
HEADS = 8
Q_LORA = 512
KV_LORA = 512
NOPE = 128
ROPE = 64
VH = 128
HQ = 256
THETA = 10000.0
SCALE = (NOPE + ROPE) ** -0.5
FW = 1024
FG = 128
N_E = 32
N_G = 8
EPG = 4
D_E = 512
EPS = 1e-6
NMOD = 6

TM = 256
TQ = 256
MOE_TM = 128
P_ROWS = 2 * S + N_E * MOE_TM
MOE_NT = P_ROWS // MOE_TM
FFT_N1 = 128
FFT_N2 = 64
COPY_CH = 1024

VMEM_LIMIT = 56 * 1024 * 1024


def _cp(sem, limit=VMEM_LIMIT):
    return pltpu.CompilerParams(dimension_semantics=sem, vmem_limit_bytes=limit)


def _const_spec(shape):
    n = len(shape)
    return pl.BlockSpec(shape, lambda *a: (0,) * n, pipeline_mode=pl.Buffered(1))


def _rms(x):
    return x * lax.rsqrt(jnp.mean(x * x, axis=-1, keepdims=True) + EPS)


def _silu(x):
    return x * jax.nn.sigmoid(x)


ADA_BN = 1024


def _ada_kernel(cb_ref, w_ref, b_ref, o_ref):
    a0 = _silu(cb_ref[0])
    a1 = _silu(cb_ref[1])
    rows = []
    for j in range(ADA_BN // 128):
        w = w_ref[0, :, j * 128:(j + 1) * 128]
        r0 = jnp.sum(w * a0, axis=0, keepdims=True)
        r1 = jnp.sum(w * a1, axis=0, keepdims=True)
        rows.append(jnp.concatenate([r0, r1, jnp.zeros((6, 128), F32)], axis=0))
    o_ref[0] = jnp.concatenate(rows, axis=1) + b_ref[0]


def _ada(c, c_ctx, ada_w, ada_b):
    depth = ada_w.shape[0]
    n = ada_w.shape[2]
    cb = jnp.stack([jnp.broadcast_to(c.reshape(D, 1), (D, 128)),
                    jnp.broadcast_to(c_ctx.reshape(D, 1), (D, 128))])
    return pl.pallas_call(
        _ada_kernel,
        grid=(depth, n // ADA_BN),
        in_specs=[pl.BlockSpec((2, D, 128), lambda l, j: (0, 0, 0)),
                  pl.BlockSpec((1, D, ADA_BN), lambda l, j: (l, 0, j)),
                  pl.BlockSpec((1, 1, ADA_BN), lambda l, j: (l, 0, j))],
        out_specs=pl.BlockSpec((1, 8, ADA_BN), lambda l, j: (l, 0, j)),
        out_shape=jax.ShapeDtypeStruct((depth, 8, n), F32),
        compiler_params=_cp(("arbitrary", "arbitrary")),
        name="ada",
    )(cb, ada_w, ada_b.reshape(depth, 1, n))


W_IN_COLS = Q_LORA + KV_LORA + 128 + FW


def _rope_chunk(x, cos, sin):
    lane = lax.broadcasted_iota(jnp.int32, x.shape, 1)
    sw = jnp.where((lane % 32) < 16, pltpu.roll(x, 112, 1), pltpu.roll(x, 16, 1))
    return x * cos + sw * sin


def _inproj_kernel(x_ref, mod_ref, g_ref, win_ref, qg_ref, wq_ref, kg_ref, wk_ref, wv_ref,
                   cos_ref, sin_ref, dft_ref, q_out, k_out, v_out, z_out):
    i = pl.program_id(0)
    is_ctx = i >= S // TM
    mod = jnp.where(is_ctx, mod_ref[0, 1:2, :], mod_ref[0, 0:1, :])
    h = _rms(x_ref[...]) * g_ref[...] * (1.0 + mod[:, D:2 * D]) + mod[:, 0:D]
    u = jnp.dot(h.astype(BF16), win_ref[...], preferred_element_type=F32)
    cos = cos_ref[...]
    sin = sin_ref[...]
    qc = (_rms(u[:, 0:Q_LORA]) * qg_ref[...]).astype(BF16)
    q = jnp.dot(qc, wq_ref[...], preferred_element_type=F32)
    for hd in range(HEADS):
        q_out[:, hd * HQ:hd * HQ + NOPE] = (q[:, hd * HQ:hd * HQ + NOPE] * SCALE).astype(BF16)
        qr = _rope_chunk(q[:, hd * HQ + NOPE:(hd + 1) * HQ], cos, sin) * SCALE
        q_out[:, hd * HQ + NOPE:(hd + 1) * HQ] = qr.astype(BF16)
    kc = (_rms(u[:, Q_LORA:Q_LORA + KV_LORA]) * kg_ref[...]).astype(BF16)
    kn = jnp.dot(kc, wk_ref[...], preferred_element_type=F32)
    v_out[...] = jnp.dot(kc, wv_ref[...], preferred_element_type=F32).astype(BF16)
    kr = _rope_chunk(u[:, 2 * Q_LORA:2 * Q_LORA + 128], cos, sin).astype(BF16)
    for hd in range(HEADS):
        k_out[:, hd * HQ:hd * HQ + NOPE] = kn[:, hd * NOPE:(hd + 1) * NOPE].astype(BF16)
        k_out[:, hd * HQ + NOPE:(hd + 1) * HQ] = kr
    f0 = 2 * Q_LORA + 128
    for g in range(FW // FG):
        fg = u[:, f0 + g * FG:f0 + (g + 1) * FG].astype(BF16)
        zz = jnp.dot(fg, dft_ref[...], preferred_element_type=F32)
        z_out[0, :, g * FG:(g + 1) * FG] = zz[:, 0:FG].astype(BF16)
        z_out[1, :, g * FG:(g + 1) * FG] = zz[:, FG:2 * FG].astype(BF16)


def _inproj(xc, mods, g1, w_in, qg, wq, kg, wk, wv, cos_t, sin_t, dft):
    nt = TT // TM
    row = lambda i: (i, 0)
    return pl.pallas_call(
        _inproj_kernel,
        grid=(nt,),
        in_specs=[pl.BlockSpec((TM, D), row),
                  pl.BlockSpec((1, 8, 2 * D), lambda i: (0, 0, 0)),
                  _const_spec((1, D)),
                  _const_spec((D, W_IN_COLS)),
                  _const_spec((1, Q_LORA)),
                  _const_spec((Q_LORA, HEADS * HQ)),
                  _const_spec((1, KV_LORA)),
                  _const_spec((KV_LORA, HEADS * NOPE)),
                  _const_spec((KV_LORA, HEADS * VH)),
                  pl.BlockSpec((TM, 128), row),
                  pl.BlockSpec((TM, 128), row),
                  _const_spec((FG, 2 * FG))],
        out_specs=[pl.BlockSpec((TM, HEADS * HQ), row),
                   pl.BlockSpec((TM, HEADS * HQ), row),
                   pl.BlockSpec((TM, HEADS * VH), row),
                   pl.BlockSpec((2, TM, FW), lambda i: (0, i, 0))],
        out_shape=[jax.ShapeDtypeStruct((TT, HEADS * HQ), BF16),
                   jax.ShapeDtypeStruct((TT, HEADS * HQ), BF16),
                   jax.ShapeDtypeStruct((TT, HEADS * VH), BF16),
                   jax.ShapeDtypeStruct((2, TT, FW), BF16)],
        compiler_params=_cp(("arbitrary",)),
        name="inproj",
    )(xc, mods, g1, w_in, qg, wq, kg, wk, wv, cos_t, sin_t, dft)


def _attn_kernel(q_ref, k_ref, v_ref, o_ref):
    s = lax.dot_general(q_ref[...], k_ref[...], (((1,), (1,)), ((), ())),
                        preferred_element_type=F32)
    m = jnp.max(s, axis=-1, keepdims=True)
    p = jnp.exp(s - m)
    l = jnp.sum(p, axis=-1, keepdims=True)
    o = jnp.dot(p.astype(BF16), v_ref[...], preferred_element_type=F32)
    o_ref[...] = (o / l).astype(BF16)


def _attention(qp, kp, v):
    return pl.pallas_call(
        _attn_kernel,
        grid=(HEADS, S // TQ),
        in_specs=[pl.BlockSpec((TQ, HQ), lambda h, i: (i, h)),
                  pl.BlockSpec((TT, HQ), lambda h, i: (0, h)),
                  pl.BlockSpec((TT, VH), lambda h, i: (0, h))],
        out_specs=pl.BlockSpec((TQ, VH), lambda h, i: (i, h)),
        out_shape=jax.ShapeDtypeStruct((S, HEADS * VH), BF16),
        compiler_params=_cp(("arbitrary", "arbitrary")),
        name="attention",
    )(qp, kp, v)


FFT1_BN = 2048


def _fft1_kernel(m_ref, z_ref, y_ref):
    y_ref[...] = jnp.dot(m_ref[...], z_ref[...], preferred_element_type=F32).astype(BF16)


def _fft2_kernel(g_ref, y_ref, o_ref):
    y = y_ref[...].reshape(2 * FFT_N2, FW)
    o_ref[0] = jnp.dot(g_ref[0], y, preferred_element_type=F32).astype(BF16)


def _fft_tables():
    k1 = np.arange(FFT_N1, dtype=np.float64)
    a = 2.0 * np.pi * np.outer(k1, k1) / FFT_N1
    c1, s1 = np.cos(a), np.sin(a)
    m1 = np.block([[c1, s1], [-s1, c1]])
    t2 = np.arange(FFT_N2, dtype=np.float64)
    k2 = np.arange(FFT_N2, dtype=np.float64)
    th = 2.0 * np.pi * (k1[:, None, None] * t2[None, None, :] / (FFT_N1 * FFT_N2)
                        + k2[None, :, None] * t2[None, None, :] / FFT_N2)
    g = np.concatenate([np.cos(th), np.sin(th)], axis=2) / np.sqrt(float(S * FG))
    cc = np.arange(FG, dtype=np.float64)
    ac = 2.0 * np.pi * np.outer(cc, cc) / FG
    dft = np.concatenate([np.cos(ac), -np.sin(ac)], axis=1)
    return (jnp.asarray(m1, F32).astype(BF16), jnp.asarray(g, F32).astype(BF16),
            jnp.asarray(dft, F32).astype(BF16))


def _fourier_tokens(z, m1, gtab):
    ncol = FFT_N2 * FW
    z2 = z[:, :S].reshape(2 * FFT_N1, ncol)
    y = pl.pallas_call(
        _fft1_kernel,
        grid=(ncol // FFT1_BN,),
        in_specs=[_const_spec((2 * FFT_N1, 2 * FFT_N1)),
                  pl.BlockSpec((2 * FFT_N1, FFT1_BN), lambda j: (0, j))],
        out_specs=pl.BlockSpec((2 * FFT_N1, FFT1_BN), lambda j: (0, j)),
        out_shape=jax.ShapeDtypeStruct((2 * FFT_N1, ncol), BF16),
        compiler_params=_cp(("arbitrary",)),
        name="fft1",
    )(m1, z2)
    y4 = y.reshape(2, FFT_N1, FFT_N2, FW)
    o = pl.pallas_call(
        _fft2_kernel,
        grid=(FFT_N1,),
        in_specs=[pl.BlockSpec((1, FFT_N2, 2 * FFT_N2), lambda k: (k, 0, 0)),
                  pl.BlockSpec((2, None, FFT_N2, FW), lambda k: (0, k, 0, 0))],
        out_specs=pl.BlockSpec((1, FFT_N2, FW), lambda k: (k, 0, 0)),
        out_shape=jax.ShapeDtypeStruct((FFT_N1, FFT_N2, FW), BF16),
        compiler_params=_cp(("arbitrary",)),
        name="fft2",
    )(gtab, y4)
    return jnp.transpose(o, (1, 0, 2)).reshape(S, FW)


def _route(h2, rw_ref, rb_ref, ids_ref, wts_ref):
    logits = jnp.dot(h2, rw_ref[...], precision=lax.Precision.HIGHEST,
                     preferred_element_type=F32)
    lt = jnp.transpose(logits)[0:N_E, :]
    sc = jax.nn.sigmoid(lt)
    bi = sc + rb_ref[...]
    sj = [sc[j * N_G:(j + 1) * N_G] for j in range(EPG)]
    bj = [bi[j * N_G:(j + 1) * N_G] for j in range(EPG)]
    m01, n01 = jnp.maximum(bj[0], bj[1]), jnp.minimum(bj[0], bj[1])
    m23, n23 = jnp.maximum(bj[2], bj[3]), jnp.minimum(bj[2], bj[3])
    gs = jnp.maximum(m01, m23) + jnp.maximum(jnp.minimum(m01, m23), jnp.maximum(n01, n23))
    gmax = jnp.max(gs, axis=0, keepdims=True)
    gi = lax.broadcasted_iota(jnp.int32, gs.shape, 0)
    best = jnp.min(jnp.where(gs == gmax, gi, N_G), axis=0, keepdims=True)
    sel = gi == best
    b = [jnp.sum(jnp.where(sel, x, 0.0), axis=0, keepdims=True) for x in bj]
    s = [jnp.sum(jnp.where(sel, x, 0.0), axis=0, keepdims=True) for x in sj]
    b1 = jnp.maximum(jnp.maximum(b[0], b[1]), jnp.maximum(b[2], b[3]))
    j1 = jnp.full(b1.shape, EPG, jnp.int32)
    for j in reversed(range(EPG)):
        j1 = jnp.where(b[j] == b1, j, j1)
    c = [jnp.where(j1 == j, -jnp.inf, b[j]) for j in range(EPG)]
    b2 = jnp.maximum(jnp.maximum(c[0], c[1]), jnp.maximum(c[2], c[3]))
    j2 = jnp.full(b1.shape, EPG, jnp.int32)
    for j in reversed(range(EPG)):
        j2 = jnp.where(c[j] == b2, j, j2)
    s1 = sum(jnp.where(j1 == j, s[j], 0.0) for j in range(EPG))
    s2 = sum(jnp.where(j2 == j, s[j], 0.0) for j in range(EPG))
    tot = s1 + s2
    zi = jnp.zeros((6,) + b1.shape[1:], jnp.int32)
    zf = jnp.zeros((6,) + b1.shape[1:], F32)
    ids_ref[...] = jnp.concatenate([best * EPG + j1, best * EPG + j2, zi], axis=0)
    wts_ref[...] = jnp.concatenate([s1 / tot, s2 / tot, zf], axis=0)


def _tail(y, x_ref, mod_ref, g2_ref, rw_ref, rb_ref, x1_ref, h2_ref, ids_ref, wts_ref):
    mod = mod_ref[0, 0:1, :]
    x1 = x_ref[...] + mod[:, 0:D] * y
    x1_ref[...] = x1
    h2 = _rms(x1) * g2_ref[...] * (1.0 + mod[:, 2 * D:3 * D]) + mod[:, D:2 * D]
    h2_ref[...] = h2
    _route(h2, rw_ref, rb_ref, ids_ref, wts_ref)


def _mix_tail_kernel(a_ref, f_ref, wa_ref, wf_ref, x_ref, mod_ref, g2_ref, rw_ref, rb_ref,
                     x1_ref, h2_ref, ids_ref, wts_ref):
    y = (jnp.dot(a_ref[...], wa_ref[...], preferred_element_type=F32)
         + jnp.dot(f_ref[...], wf_ref[...], preferred_element_type=F32))
    _tail(y, x_ref, mod_ref, g2_ref, rw_ref, rb_ref, x1_ref, h2_ref, ids_ref, wts_ref)


def _tail_specs():
    row = lambda i: (i, 0)
    in_specs = [pl.BlockSpec((TM, D), row),
                pl.BlockSpec((1, 8, 3 * D), lambda i: (0, 0, 0)),
                _const_spec((1, D)),
                _const_spec((D, 128)),
                _const_spec((N_E, 1))]
    out_specs = [pl.BlockSpec((TM, D), row),
                 pl.BlockSpec((TM, D), row),
                 pl.BlockSpec((8, TM), lambda i: (0, i)),
                 pl.BlockSpec((8, TM), lambda i: (0, i))]
    out_shape = [jax.ShapeDtypeStruct((S, D), F32),
                 jax.ShapeDtypeStruct((S, D), F32),
                 jax.ShapeDtypeStruct((8, S), jnp.int32),
                 jax.ShapeDtypeStruct((8, S), F32)]
    return in_specs, out_specs, out_shape


def _mix_tail(attn, four, wa, wf, x, mod, g2, rw, rb):
    row = lambda i: (i, 0)
    tin, tout, tshape = _tail_specs()
    return pl.pallas_call(
        _mix_tail_kernel,
        grid=(S // TM,),
        in_specs=[pl.BlockSpec((TM, HEADS * VH), row),
                  pl.BlockSpec((TM, FW), row),
                  _const_spec((HEADS * VH, D)),
                  _const_spec((FW, D))] + tin,
        out_specs=tout,
        out_shape=tshape,
        compiler_params=_cp(("arbitrary",)),
        name="mix_tail",
    )(attn, four, wa, wf, x, mod, g2, rw, rb)


def _conv_in_kernel(h_ref, w_ref, b_ref, cz_ref):
    u = jnp.dot(h_ref[...], w_ref[...], preferred_element_type=F32)
    b_ref[...] = u[:, 0:D].astype(BF16)
    cz_ref[...] = (u[:, D:2 * D] * u[:, 2 * D:3 * D]).astype(BF16)


def _conv_in(h, w):
    row = lambda i: (i, 0)
    return pl.pallas_call(
        _conv_in_kernel,
        grid=(S // TM,),
        in_specs=[pl.BlockSpec((TM, D), row), _const_spec((D, 3 * D))],
        out_specs=[pl.BlockSpec((TM, D), row), pl.BlockSpec((TM, D), row)],
        out_shape=[jax.ShapeDtypeStruct((S, D), BF16), jax.ShapeDtypeStruct((S, D), BF16)],
        compiler_params=_cp(("arbitrary",)),
        name="conv_in",
    )(h, w)


def _conv_tail_kernel(b_ref, cz_ref, prev_ref, next_ref, cw_ref, wo_ref, x_ref, mod_ref, g2_ref,
                      rw_ref, rb_ref, x1_ref, h2_ref, ids_ref, wts_ref):
    i = pl.program_id(0)
    cz = cz_ref[...].astype(F32)
    rid = lax.broadcasted_iota(jnp.int32, cz.shape, 0)
    prev_row = jnp.where(i > 0, prev_ref[15:16, :].astype(F32), 0.0)
    next_row = jnp.where(i < S // TM - 1, next_ref[0:1, :].astype(F32), 0.0)
    dn = jnp.where(rid == 0, prev_row, pltpu.roll(cz, 1, 0))
    up = jnp.where(rid == TM - 1, next_row, pltpu.roll(cz, TM - 1, 0))
    y = dn * cw_ref[0:1, :] + cz * cw_ref[1:2, :] + up * cw_ref[2:3, :]
    yb = (b_ref[...].astype(F32) * y).astype(BF16)
    y = jnp.dot(yb, wo_ref[...], preferred_element_type=F32)
    _tail(y, x_ref, mod_ref, g2_ref, rw_ref, rb_ref, x1_ref, h2_ref, ids_ref, wts_ref)


def _conv_tail(bg, cz, cw, wo, x, mod, g2, rw, rb):
    row = lambda i: (i, 0)
    tin, tout, tshape = _tail_specs()
    nb = TM // 16
    return pl.pallas_call(
        _conv_tail_kernel,
        grid=(S // TM,),
        in_specs=[pl.BlockSpec((TM, D), row),
                  pl.BlockSpec((TM, D), row),
                  pl.BlockSpec((16, D), lambda i: (jnp.maximum(i * nb - 1, 0), 0)),
                  pl.BlockSpec((16, D), lambda i: (jnp.minimum((i + 1) * nb, S // 16 - 1), 0)),
                  _const_spec((8, D)),
                  _const_spec((D, D))] + tin,
        out_specs=tout,
        out_shape=tshape,
        compiler_params=_cp(("arbitrary",)),
        name="conv_tail",
    )(bg, cz, cz, cz, cw, wo, x, mod, g2, rw, rb)


def _row_copy_kernel(sidx_ref, didx_ref, src_ref, *rest):
    dst_ref, sem = rest[-2], rest[-1]

    def row_dma(s, d):
        return pltpu.make_async_copy(src_ref.at[pl.ds(s, 1)], dst_ref.at[pl.ds(d, 1)], sem)

    def issue(j, carry):
        row_dma(sidx_ref[0, j], didx_ref[0, j]).start()
        return carry

    def drain(j, carry):
        row_dma(sidx_ref[0, j], didx_ref[0, j]).wait()
        return carry

    lax.fori_loop(0, COPY_CH, issue, 0, unroll=8)
    lax.fori_loop(0, COPY_CH, drain, 0, unroll=8)


def _row_copy(src, sidx, didx, n_dst, dst_init=None):
    n = sidx.shape[0]
    nch = n // COPY_CH
    smem = lambda: pl.BlockSpec((None, 1, COPY_CH), lambda i: (i, 0, 0), memory_space=pltpu.SMEM)
    extra = [] if dst_init is None else [dst_init]
    return pl.pallas_call(
        _row_copy_kernel,
        grid=(nch,),
        in_specs=[smem(), smem()] + [pl.BlockSpec(memory_space=pl.ANY)] * (1 + len(extra)),
        out_specs=pl.BlockSpec(memory_space=pl.ANY),
        out_shape=jax.ShapeDtypeStruct((n_dst, src.shape[1]), src.dtype),
        scratch_shapes=[pltpu.SemaphoreType.DMA(())],
        input_output_aliases={} if dst_init is None else {3: 0},
        compiler_params=pltpu.CompilerParams(dimension_semantics=("arbitrary",),
                                             has_side_effects=True),
        name="row_copy",
    )(sidx.reshape(nch, 1, COPY_CH), didx.reshape(nch, 1, COPY_CH), src, *extra)


def _moe_kernel(te_ref, first_ref, nused_ref, x_ref, wg_ref, wu_ref, wd_ref, o_ref,
                wg_s, wu_s, wd_s):
    i = pl.program_id(0)

    @pl.when(first_ref[i] == 1)
    def _():
        wg_s[...] = wg_ref[...].astype(BF16)
        wu_s[...] = wu_ref[...].astype(BF16)
        wd_s[...] = wd_ref[...].astype(BF16)

    @pl.when(i < nused_ref[0])
    def _():
        x = x_ref[...].astype(BF16)
        g = jnp.dot(x, wg_s[...], preferred_element_type=F32)
        u = jnp.dot(x, wu_s[...], preferred_element_type=F32)
        a = (_silu(g) * u).astype(BF16)
        o_ref[...] = jnp.dot(a, wd_s[...], preferred_element_type=F32)

    @pl.when(i >= nused_ref[0])
    def _():
        o_ref[...] = jnp.zeros(o_ref.shape, F32)


def _moe(te, first, nused, xs, w_gate, w_up, w_down, layer):
    def xmap(i, te, first, nused):
        return (jnp.minimum(i, nused[0] - 1), 0)

    def omap(i, te, first, nused):
        return (i, 0)

    wmap = lambda i, te, first, nused: (layer, te[i], 0, 0)
    grid_spec = pltpu.PrefetchScalarGridSpec(
        num_scalar_prefetch=3,
        grid=(MOE_NT,),
        in_specs=[pl.BlockSpec((MOE_TM, D), xmap),
                  pl.BlockSpec((None, None, D, D_E), wmap),
                  pl.BlockSpec((None, None, D, D_E), wmap),
                  pl.BlockSpec((None, None, D_E, D), wmap)],
        out_specs=pl.BlockSpec((MOE_TM, D), omap),
        scratch_shapes=[pltpu.VMEM((D, D_E), BF16), pltpu.VMEM((D, D_E), BF16),
                        pltpu.VMEM((D_E, D), BF16)],
    )
    return pl.pallas_call(
        _moe_kernel,
        grid_spec=grid_spec,
        out_shape=jax.ShapeDtypeStruct((P_ROWS, D), F32),
        compiler_params=_cp(("arbitrary",)),
        name="moe",
    )(te, first, nused, xs, w_gate, w_up, w_down)


def _route_plan(ids):
    ef = ids[0:2].T.reshape(-1)
    oh = (ef[:, None] == jnp.arange(N_E, dtype=jnp.int32)[None, :]).astype(jnp.int32)
    csum = jnp.cumsum(oh, axis=0)
    rank = jnp.sum(csum * oh, axis=1) - 1
    cnt = csum[-1]
    pc = ((cnt + MOE_TM - 1) // MOE_TM) * MOE_TM
    end = jnp.cumsum(pc)
    off = end - pc
    pos = jnp.sum(oh * off[None, :], axis=1) + rank
    nused = (end[-1] // MOE_TM).astype(jnp.int32)
    ti = jnp.arange(MOE_NT, dtype=jnp.int32)
    te = jnp.sum((ti[:, None] >= (end // MOE_TM)[None, :]).astype(jnp.int32), axis=1)
    te_last = jnp.sum((nused - 1 >= end // MOE_TM).astype(jnp.int32))
    te = jnp.minimum(te, te_last).astype(jnp.int32)
    first = jnp.concatenate([jnp.ones((1,), jnp.int32), (te[1:] != te[:-1]).astype(jnp.int32)])
    return pos.astype(jnp.int32), te, first, nused.reshape(1)


def _moe_block(h2, ids, w_gate, w_up, w_down, layer):
    pos, te, first, nused = _route_plan(ids)
    tok = jnp.arange(2 * S, dtype=jnp.int32) // 2
    xs = _row_copy(h2, tok, pos, P_ROWS, jnp.zeros((P_ROWS, D), F32))
    ys = _moe(te, first, nused, xs, w_gate, w_up, w_down, layer)
    back = (jnp.arange(2 * S, dtype=jnp.int32) % 2) * S + tok
    return _row_copy(ys, pos, back, 2 * S)


def _combine(x1_ref, y0_ref, y1_ref, w_ref, mod_ref):
    w = w_ref[...]
    moe = w[:, 0:1] * y0_ref[...] + w[:, 1:2] * y1_ref[...]
    return x1_ref[...] + mod_ref[0, 0:1, :] * moe


def _fuse_next_kernel(x1_ref, y0_ref, y1_ref, w_ref, mod_ref, nmod_ref, g_ref, x2_ref, h_ref):
    x2 = _combine(x1_ref, y0_ref, y1_ref, w_ref, mod_ref)
    x2_ref[...] = x2
    nm = nmod_ref[0, 0:1, :]
    h_ref[...] = (_rms(x2) * g_ref[...] * (1.0 + nm[:, D:2 * D]) + nm[:, 0:D]).astype(BF16)


def _fuse_final_kernel(x1_ref, y0_ref, y1_ref, w_ref, mod_ref, g_ref, o_ref):
    x2 = _combine(x1_ref, y0_ref, y1_ref, w_ref, mod_ref)
    o_ref[...] = _rms(x2) * g_ref[...]


def _fuse_specs():
    row = lambda i: (i, 0)
    return [pl.BlockSpec((TM, D), row),
            pl.BlockSpec((TM, D), row),
            pl.BlockSpec((TM, D), lambda i: (i + S // TM, 0)),
            pl.BlockSpec((TM, 8), row)]


def _fuse_next(x1, yg, wt, mod_g2, nmod, g):
    row = lambda i: (i, 0)
    return pl.pallas_call(
        _fuse_next_kernel,
        grid=(S // TM,),
        in_specs=_fuse_specs() + [pl.BlockSpec((1, 8, D), lambda i: (0, 0, 0)),
                                  pl.BlockSpec((1, 8, 2 * D), lambda i: (0, 0, 0)),
                                  _const_spec((1, D))],
        out_specs=[pl.BlockSpec((TM, D), row), pl.BlockSpec((TM, D), row)],
        out_shape=[jax.ShapeDtypeStruct((S, D), F32), jax.ShapeDtypeStruct((S, D), BF16)],
        compiler_params=_cp(("arbitrary",)),
        name="fuse_next",
    )(x1, yg, yg, wt, mod_g2, nmod, g)


def _fuse_final(x1, yg, wt, mod_g2, g):
    row = lambda i: (i, 0)
    return pl.pallas_call(
        _fuse_final_kernel,
        grid=(S // TM,),
        in_specs=_fuse_specs() + [pl.BlockSpec((1, 8, D), lambda i: (0, 0, 0)),
                                  _const_spec((1, D))],
        out_specs=pl.BlockSpec((TM, D), row),
        out_shape=jax.ShapeDtypeStruct((S, D), F32),
        compiler_params=_cp(("arbitrary",)),
        name="fuse_final",
    )(x1, yg, yg, wt, mod_g2, g)


def _rope_tables():
    half = ROPE // 2
    row = jnp.repeat(jnp.arange(S // GRID_W), GRID_W).astype(F32)
    col = jnp.tile(jnp.arange(GRID_W), S // GRID_W).astype(F32)
    inv = THETA ** (-jnp.arange(0, half, 2, dtype=F32) / half)
    ang_r = row[:, None] * inv
    ang_c = col[:, None] * inv
    cr, sr, cc, sc = jnp.cos(ang_r), jnp.sin(ang_r), jnp.cos(ang_c), jnp.sin(ang_c)
    one = jnp.ones((S, 64), F32)
    zero = jnp.zeros((S, 64), F32)
    cos_t = jnp.concatenate([cr, cr, cc, cc, one], axis=1)
    sin_t = jnp.concatenate([-sr, sr, -sc, sc, zero], axis=1)
    cos_t = jnp.concatenate([cos_t, jnp.ones((CTX, 128), F32)], axis=0)
    sin_t = jnp.concatenate([sin_t, jnp.zeros((CTX, 128), F32)], axis=0)
    return cos_t, sin_t


def _prep_layer0(attn_in_w, q_up_w, kv_up_w):
    c0, c1, c2 = Q_LORA, Q_LORA + KV_LORA, Q_LORA + KV_LORA + ROPE
    w_in = jnp.concatenate([attn_in_w[:, :c2], jnp.zeros((D, 128 - ROPE), F32), attn_in_w[:, c2:]],
                           axis=1).astype(BF16)
    qw = q_up_w.reshape(Q_LORA, HEADS, NOPE + ROPE)
    wq = jnp.concatenate([qw, jnp.zeros((Q_LORA, HEADS, HQ - NOPE - ROPE), F32)],
                         axis=2).reshape(Q_LORA, HEADS * HQ).astype(BF16)
    kvw = kv_up_w.reshape(KV_LORA, HEADS, NOPE + VH)
    wk = kvw[:, :, :NOPE].reshape(KV_LORA, HEADS * NOPE).astype(BF16)
    wv = kvw[:, :, NOPE:].reshape(KV_LORA, HEADS * VH).astype(BF16)
    del c0, c1
    return w_in, wq, wk, wv


def _prep_router(router_w, router_b):
    perm = np.array([EPG * (s % N_G) + s // N_G for s in range(N_E)])
    rw = jnp.concatenate([router_w[:, perm], jnp.zeros((D, 128 - N_E), F32)], axis=1)
    rb = router_b[perm].reshape(N_E, 1)
    return rw, rb


def kernel(x, c, ctx, c_ctx, ada_w, ada_b, norm1_g, norm2_g, attn_in_w, q_norm_g, q_up_w, kv_norm_g,
           kv_up_w, conv_in_w, conv_w, mix_out_w, router_w, router_b, moe_w_gate, moe_w_up,
           moe_w_down, final_norm_g):
    x = x.reshape(S, D)
    mods = _ada(c.reshape(D), c_ctx, ada_w, ada_b)
    rw, rb = _prep_router(router_w, router_b)
    m1, gtab, dft = _fft_tables()
    cos_t, sin_t = _rope_tables()

    w_in, wq, wk, wv = _prep_layer0(attn_in_w[0], q_up_w[0], kv_up_w[0])
    xc = jnp.concatenate([x, ctx.reshape(CTX, D)], axis=0)
    qp, kp, v, z = _inproj(xc, mods[0:1, :, 0:2 * D], norm1_g[0:1], w_in, q_norm_g[0:1], wq,
                           kv_norm_g[0:1], wk, wv, cos_t, sin_t, dft)
    attn = _attention(qp, kp, v)
    four = _fourier_tokens(z, m1, gtab)
    wo = mix_out_w[0].astype(BF16)
    x1, h2, ids, wts = _mix_tail(attn, four, wo[:HEADS * VH], wo[HEADS * VH:], x,
                                 mods[0:1, :, 2 * D:5 * D], norm2_g[0:1], rw, rb)
    yg = _moe_block(h2, ids, moe_w_gate, moe_w_up, moe_w_down, 0)
    x2, h = _fuse_next(x1, yg, wts.T, mods[0:1, :, 5 * D:6 * D], mods[1:2, :, 0:2 * D], norm1_g[1:2])

    bg, cz = _conv_in(h, conv_in_w[0].astype(BF16))
    cw = jnp.concatenate([conv_w[0], jnp.zeros((5, D), F32)], axis=0)
    x1, h2, ids, wts = _conv_tail(bg, cz, cw, mix_out_w[1].astype(BF16), x2,
                                  mods[1:2, :, 2 * D:5 * D], norm2_g[1:2], rw, rb)
    yg = _moe_block(h2, ids, moe_w_gate, moe_w_up, moe_w_down, 1)
    out = _fuse_final(x1, yg, wts.T, mods[1:2, :, 5 * D:6 * D], final_norm_g.reshape(1, D))
    return out.reshape(1, S, D)
```

```python
import functools

import numpy as np
import jax
import jax.numpy as jnp
from jax import lax
from jax.experimental import pallas as pl
from jax.experimental.pallas import tpu as pltpu

F32 = jnp.float32
BF16 = jnp.bfloat16

D = 2048
S = 8192
CTX = 256
TT = S + CTX
GRID_W = 64
HEADS = 8
Q_LORA = 512
KV_LORA = 512
NOPE = 128
ROPE = 64
VH = 128
HQ = 256
THETA = 10000.0
SCALE = (NOPE + ROPE) ** -0.5
FW = 1024
FG = 128
N_E = 32
N_G = 8
EPG = 4
D_E = 512
EPS = 1e-6
NMOD = 6

TM = 256
TQ = 256
MOE_TM = 128
P_ROWS = 2 * S + N_E * MOE_TM
MOE_NT = P_ROWS // MOE_TM
FFT_N1 = 128
FFT_N2 = 64
COPY_CH = 1024

VMEM_LIMIT = 56 * 1024 * 1024


def _cp(sem, limit=VMEM_LIMIT):
    return pltpu.CompilerParams(dimension_semantics=sem, vmem_limit_bytes=limit)


def _const_spec(shape):
    n = len(shape)
    return pl.BlockSpec(shape, lambda *a: (0,) * n, pipeline_mode=pl.Buffered(1))


SLABS = D // 128


def _load_rows(ref, n):
    return jnp.concatenate([ref[pl.ds(j, n, stride=SLABS), :] for j in range(SLABS)], axis=1)


def _store_rows(ref, val, n):
    for j in range(SLABS):
        ref[pl.ds(j, n, stride=SLABS), :] = val[:, j * 128:(j + 1) * 128]


def _rms(x):
    return x * lax.rsqrt(jnp.mean(x * x, axis=-1, keepdims=True) + EPS)


def _silu(x):
    return x * jax.nn.sigmoid(x)


ADA_BN = 1024


def _ada_kernel(cb_ref, w_ref, b_ref, o_ref):
    a0 = _silu(cb_ref[0])
    a1 = _silu(cb_ref[1])
    rows = []
    for j in range(ADA_BN // 128):
        w = w_ref[0, :, j * 128:(j + 1) * 128]
        r0 = jnp.sum(w * a0, axis=0, keepdims=True)
        r1 = jnp.sum(w * a1, axis=0, keepdims=True)
        rows.append(jnp.concatenate([r0, r1, jnp.zeros((6, 128), F32)], axis=0))
    o_ref[0] = jnp.concatenate(rows, axis=1) + b_ref[0]


def _ada(c, c_ctx, ada_w, ada_b):
    depth = ada_w.shape[0]
    n = ada_w.shape[2]
    cb = jnp.stack([jnp.broadcast_to(c.reshape(D, 1), (D, 128)),
                    jnp.broadcast_to(c_ctx.reshape(D, 1), (D, 128))])
    return pl.pallas_call(
        _ada_kernel,
        grid=(depth, n // ADA_BN),
        in_specs=[pl.BlockSpec((2, D, 128), lambda l, j: (0, 0, 0)),
                  pl.BlockSpec((1, D, ADA_BN), lambda l, j: (l, 0, j)),
                  pl.BlockSpec((1, 1, ADA_BN), lambda l, j: (l, 0, j))],
        out_specs=pl.BlockSpec((1, 8, ADA_BN), lambda l, j: (l, 0, j)),
        out_shape=jax.ShapeDtypeStruct((depth, 8, n), F32),
        compiler_params=_cp(("arbitrary", "arbitrary")),
        name="ada",
    )(cb, ada_w, ada_b.reshape(depth, 1, n))


W_IN_COLS = Q_LORA + KV_LORA + 128 + FW


def _rope_chunk(x, cos, sin):
    lane = lax.broadcasted_iota(jnp.int32, x.shape, 1)
    sw = jnp.where((lane % 32) < 16, pltpu.roll(x, 112, 1), pltpu.roll(x, 16, 1))
    return x * cos + sw * sin


def _inproj_kernel(x_ref, mod_ref, g_ref, win_ref, qg_ref, wq_ref, kg_ref, wk_ref, wv_ref,
                   cos_ref, sin_ref, dft_ref, q_out, k_out, v_out, z_out):
    i = pl.program_id(0)
    is_ctx = i >= S // TM
    mod = jnp.where(is_ctx, mod_ref[0, 1:2, :], mod_ref[0, 0:1, :])
    h = _rms(x_ref[...]) * g_ref[...] * (1.0 + mod[:, D:2 * D]) + mod[:, 0:D]
    u = jnp.dot(h.astype(BF16), win_ref[...], preferred_element_type=F32)
    cos = cos_ref[...]
    sin = sin_ref[...]
    qc = (_rms(u[:, 0:Q_LORA]) * qg_ref[...]).astype(BF16)
    q = jnp.dot(qc, wq_ref[...], preferred_element_type=F32)
    for hd in range(HEADS):
        q_out[:, hd * HQ:hd * HQ + NOPE] = (q[:, hd * HQ:hd * HQ + NOPE] * SCALE).astype(BF16)
        qr = _rope_chunk(q[:, hd * HQ + NOPE:(hd + 1) * HQ], cos, sin) * SCALE
        q_out[:, hd * HQ + NOPE:(hd + 1) * HQ] = qr.astype(BF16)
    kc = (_rms(u[:, Q_LORA:Q_LORA + KV_LORA]) * kg_ref[...]).astype(BF16)
    kn = jnp.dot(kc, wk_ref[...], preferred_element_type=F32)
    v_out[...] = jnp.dot(kc, wv_ref[...], preferred_element_type=F32).astype(BF16)
    kr = _rope_chunk(u[:, 2 * Q_LORA:2 * Q_LORA + 128], cos, sin).astype(BF16)
    for hd in range(HEADS):
        k_out[:, hd * HQ:hd * HQ + NOPE] = kn[:, hd * NOPE:(hd + 1) * NOPE].astype(BF16)
        k_out[:, hd * HQ + NOPE:(hd + 1) * HQ] = kr
    f0 = 2 * Q_LORA + 128
    for g in range(FW // FG):
        fg = u[:, f0 + g * FG:f0 + (g + 1) * FG].astype(BF16)
        zz = jnp.dot(fg, dft_ref[...], preferred_element_type=F32)
        z_out[0, :, g * FG:(g + 1) * FG] = zz[:, 0:FG].astype(BF16)
        z_out[1, :, g * FG:(g + 1) * FG] = zz[:, FG:2 * FG].astype(BF16)


def _inproj(xc, mods, g1, w_in, qg, wq, kg, wk, wv, cos_t, sin_t, dft):
    nt = TT // TM
    row = lambda i: (i, 0)
    return pl.pallas_call(
        _inproj_kernel,
        grid=(nt,),
        in_specs=[pl.BlockSpec((TM, D), row),
                  pl.BlockSpec((1, 8, 2 * D), lambda i: (0, 0, 0)),
                  _const_spec((1, D)),
                  _const_spec((D, W_IN_COLS)),
                  _const_spec((1, Q_LORA)),
                  _const_spec((Q_LORA, HEADS * HQ)),
                  _const_spec((1, KV_LORA)),
                  _const_spec((KV_LORA, HEADS * NOPE)),
                  _const_spec((KV_LORA, HEADS * VH)),
                  pl.BlockSpec((TM, 128), row),
                  pl.BlockSpec((TM, 128), row),
                  _const_spec((FG, 2 * FG))],
        out_specs=[pl.BlockSpec((TM, HEADS * HQ), row),
                   pl.BlockSpec((TM, HEADS * HQ), row),
                   pl.BlockSpec((TM, HEADS * VH), row),
                   pl.BlockSpec((2, TM, FW), lambda i: (0, i, 0))],
        out_shape=[jax.ShapeDtypeStruct((TT, HEADS * HQ), BF16),
                   jax.ShapeDtypeStruct((TT, HEADS * HQ), BF16),
                   jax.ShapeDtypeStruct((TT, HEADS * VH), BF16),
                   jax.ShapeDtypeStruct((2, TT, FW), BF16)],
        compiler_params=_cp(("arbitrary",)),
        name="inproj",
    )(xc, mods, g1, w_in, qg, wq, kg, wk, wv, cos_t, sin_t, dft)


def _attn_kernel(q_ref, k_ref, v_ref, o_ref):
    s = lax.dot_general(q_ref[...], k_ref[...], (((1,), (1,)), ((), ())),
                        preferred_element_type=F32)
    m = jnp.max(s, axis=-1, keepdims=True)
    p = jnp.exp(s - m)
    l = jnp.sum(p, axis=-1, keepdims=True)
    o = jnp.dot(p.astype(BF16), v_ref[...], preferred_element_type=F32)
    o_ref[...] = (o / l).astype(BF16)


def _attention(qp, kp, v):
    return pl.pallas_call(
        _attn_kernel,
        grid=(HEADS, S // TQ),
        in_specs=[pl.BlockSpec((TQ, HQ), lambda h, i: (i, h)),
                  pl.BlockSpec((TT, HQ), lambda h, i: (0, h)),
                  pl.BlockSpec((TT, VH), lambda h, i: (0, h))],
        out_specs=pl.BlockSpec((TQ, VH), lambda h, i: (i, h)),
        out_shape=jax.ShapeDtypeStruct((S, HEADS * VH), BF16),
        compiler_params=_cp(("arbitrary", "arbitrary")),
        name="attention",
    )(qp, kp, v)


FFT1_BN = 2048


def _fft1_kernel(m_ref, z_ref, y_ref):
    y_ref[...] = jnp.dot(m_ref[...], z_ref[...], preferred_element_type=F32).astype(BF16)


def _fft2_kernel(g_ref, y_ref, o_ref):
    y = y_ref[...].reshape(2 * FFT_N2, FW)
    o_ref[0] = jnp.dot(g_ref[0], y, preferred_element_type=F32).astype(BF16)


def _fft_tables():
    k1 = np.arange(FFT_N1, dtype=np.float64)
    a = 2.0 * np.pi * np.outer(k1, k1) / FFT_N1
    c1, s1 = np.cos(a), np.sin(a)
    m1 = np.block([[c1, s1], [-s1, c1]])
    t2 = np.arange(FFT_N2, dtype=np.float64)
    k2 = np.arange(FFT_N2, dtype=np.float64)
    th = 2.0 * np.pi * (k1[:, None, None] * t2[None, None, :] / (FFT_N1 * FFT_N2)
                        + k2[None, :, None] * t2[None, None, :] / FFT_N2)
    g = np.concatenate([np.cos(th), np.sin(th)], axis=2) / np.sqrt(float(S * FG))
    cc = np.arange(FG, dtype=np.float64)
    ac = 2.0 * np.pi * np.outer(cc, cc) / FG
    dft = np.concatenate([np.cos(ac), -np.sin(ac)], axis=1)
    return (jnp.asarray(m1, F32).astype(BF16), jnp.asarray(g, F32).astype(BF16),
            jnp.asarray(dft, F32).astype(BF16))


def _fourier_tokens(z, m1, gtab):
    ncol = FFT_N2 * FW
    z2 = z[:, :S].reshape(2 * FFT_N1, ncol)
    y = pl.pallas_call(
        _fft1_kernel,
        grid=(ncol // FFT1_BN,),
        in_specs=[_const_spec((2 * FFT_N1, 2 * FFT_N1)),
                  pl.BlockSpec((2 * FFT_N1, FFT1_BN), lambda j: (0, j))],
        out_specs=pl.BlockSpec((2 * FFT_N1, FFT1_BN), lambda j: (0, j)),
        out_shape=jax.ShapeDtypeStruct((2 * FFT_N1, ncol), BF16),
        compiler_params=_cp(("arbitrary",)),
        name="fft1",
    )(m1, z2)
    y4 = y.reshape(2, FFT_N1, FFT_N2, FW)
    o = pl.pallas_call(
        _fft2_kernel,
        grid=(FFT_N1,),
        in_specs=[pl.BlockSpec((1, FFT_N2, 2 * FFT_N2), lambda k: (k, 0, 0)),
                  pl.BlockSpec((2, None, FFT_N2, FW), lambda k: (0, k, 0, 0))],
        out_specs=pl.BlockSpec((1, FFT_N2, FW), lambda k: (k, 0, 0)),
        out_shape=jax.ShapeDtypeStruct((FFT_N1, FFT_N2, FW), BF16),
        compiler_params=_cp(("arbitrary",)),
        name="fft2",
    )(gtab, y4)
    return jnp.transpose(o, (1, 0, 2)).reshape(S, FW)


def _route(h2, rw_ref, rb_ref, ids_ref, wts_ref):
    logits = jnp.dot(h2, rw_ref[...], precision=lax.Precision.HIGHEST,
                     preferred_element_type=F32)
    lt = jnp.transpose(logits)[0:N_E, :]
    sc = jax.nn.sigmoid(lt)
    bi = sc + rb_ref[...]
    sj = [sc[j * N_G:(j + 1) * N_G] for j in range(EPG)]
    bj = [bi[j * N_G:(j + 1) * N_G] for j in range(EPG)]
    m01, n01 = jnp.maximum(bj[0], bj[1]), jnp.minimum(bj[0], bj[1])
    m23, n23 = jnp.maximum(bj[2], bj[3]), jnp.minimum(bj[2], bj[3])
    gs = jnp.maximum(m01, m23) + jnp.maximum(jnp.minimum(m01, m23), jnp.maximum(n01, n23))
    gmax = jnp.max(gs, axis=0, keepdims=True)
    gi = lax.broadcasted_iota(jnp.int32, gs.shape, 0)
    best = jnp.min(jnp.where(gs == gmax, gi, N_G), axis=0, keepdims=True)
    sel = gi == best
    b = [jnp.sum(jnp.where(sel, x, 0.0), axis=0, keepdims=True) for x in bj]
    s = [jnp.sum(jnp.where(sel, x, 0.0), axis=0, keepdims=True) for x in sj]
    b1 = jnp.maximum(jnp.maximum(b[0], b[1]), jnp.maximum(b[2], b[3]))
    j1 = jnp.full(b1.shape, EPG, jnp.int32)
    for j in reversed(range(EPG)):
        j1 = jnp.where(b[j] == b1, j, j1)
    c = [jnp.where(j1 == j, -jnp.inf, b[j]) for j in range(EPG)]
    b2 = jnp.maximum(jnp.maximum(c[0], c[1]), jnp.maximum(c[2], c[3]))
    j2 = jnp.full(b1.shape, EPG, jnp.int32)
    for j in reversed(range(EPG)):
        j2 = jnp.where(c[j] == b2, j, j2)
    s1 = sum(jnp.where(j1 == j, s[j], 0.0) for j in range(EPG))
    s2 = sum(jnp.where(j2 == j, s[j], 0.0) for j in range(EPG))
    tot = s1 + s2
    zi = jnp.zeros((6,) + b1.shape[1:], jnp.int32)
    zf = jnp.zeros((6,) + b1.shape[1:], F32)
    ids_ref[...] = jnp.concatenate([best * EPG + j1, best * EPG + j2, zi], axis=0)
    wts_ref[...] = jnp.concatenate([s1 / tot, s2 / tot, zf], axis=0)


def _tail(y, x_ref, mod_ref, g2_ref, rw_ref, rb_ref, x1_ref, h2_ref, ids_ref, wts_ref):
    mod = mod_ref[0, 0:1, :]
    x1 = x_ref[...] + mod[:, 0:D] * y
    x1_ref[...] = x1
    h2 = _rms(x1) * g2_ref[...] * (1.0 + mod[:, 2 * D:3 * D]) + mod[:, D:2 * D]
    _store_rows(h2_ref, h2, TM)
    _route(h2, rw_ref, rb_ref, ids_ref, wts_ref)


def _mix_tail_kernel(a_ref, f_ref, wa_ref, wf_ref, x_ref, mod_ref, g2_ref, rw_ref, rb_ref,
                     x1_ref, h2_ref, ids_ref, wts_ref):
    y = (jnp.dot(a_ref[...], wa_ref[...], preferred_element_type=F32)
         + jnp.dot(f_ref[...], wf_ref[...], preferred_element_type=F32))
    _tail(y, x_ref, mod_ref, g2_ref, rw_ref, rb_ref, x1_ref, h2_ref, ids_ref, wts_ref)


def _tail_specs():
    row = lambda i: (i, 0)
    in_specs = [pl.BlockSpec((TM, D), row),
                pl.BlockSpec((1, 8, 3 * D), lambda i: (0, 0, 0)),
                _const_spec((1, D)),
                _const_spec((D, 128)),
                _const_spec((N_E, 1))]
    out_specs = [pl.BlockSpec((TM, D), row),
                 pl.BlockSpec((TM * SLABS, 128), row),
                 pl.BlockSpec((8, TM), lambda i: (0, i)),
                 pl.BlockSpec((8, TM), lambda i: (0, i))]
    out_shape = [jax.ShapeDtypeStruct((S, D), F32),
                 jax.ShapeDtypeStruct((S * SLABS, 128), F32),
                 jax.ShapeDtypeStruct((8, S), jnp.int32),
                 jax.ShapeDtypeStruct((8, S), F32)]
    return in_specs, out_specs, out_shape


def _mix_tail(attn, four, wa, wf, x, mod, g2, rw, rb):
    row = lambda i: (i, 0)
    tin, tout, tshape = _tail_specs()
    return pl.pallas_call(
        _mix_tail_kernel,
        grid=(S // TM,),
        in_specs=[pl.BlockSpec((TM, HEADS * VH), row),
                  pl.BlockSpec((TM, FW), row),
                  _const_spec((HEADS * VH, D)),
                  _const_spec((FW, D))] + tin,
        out_specs=tout,
        out_shape=tshape,
        compiler_params=_cp(("arbitrary",)),
        name="mix_tail",
    )(attn, four, wa, wf, x, mod, g2, rw, rb)


def _conv_in_kernel(h_ref, w_ref, b_ref, cz_ref):
    u = jnp.dot(h_ref[...], w_ref[...], preferred_element_type=F32)
    b_ref[...] = u[:, 0:D].astype(BF16)
    cz_ref[...] = (u[:, D:2 * D] * u[:, 2 * D:3 * D]).astype(BF16)


def _conv_in(h, w):
    row = lambda i: (i, 0)
    return pl.pallas_call(
        _conv_in_kernel,
        grid=(S // TM,),
        in_specs=[pl.BlockSpec((TM, D), row), _const_spec((D, 3 * D))],
        out_specs=[pl.BlockSpec((TM, D), row), pl.BlockSpec((TM, D), row)],
        out_shape=[jax.ShapeDtypeStruct((S, D), BF16), jax.ShapeDtypeStruct((S, D), BF16)],
        compiler_params=_cp(("arbitrary",)),
        name="conv_in",
    )(h, w)


def _conv_tail_kernel(b_ref, cz_ref, prev_ref, next_ref, cw_ref, wo_ref, x_ref, mod_ref, g2_ref,
                      rw_ref, rb_ref, x1_ref, h2_ref, ids_ref, wts_ref):
    i = pl.program_id(0)
    cz = cz_ref[...].astype(F32)
    rid = lax.broadcasted_iota(jnp.int32, cz.shape, 0)
    prev_row = jnp.where(i > 0, prev_ref[15:16, :].astype(F32), 0.0)
    next_row = jnp.where(i < S // TM - 1, next_ref[0:1, :].astype(F32), 0.0)
    dn = jnp.where(rid == 0, prev_row, pltpu.roll(cz, 1, 0))
    up = jnp.where(rid == TM - 1, next_row, pltpu.roll(cz, TM - 1, 0))
    y = dn * cw_ref[0:1, :] + cz * cw_ref[1:2, :] + up * cw_ref[2:3, :]
    yb = (b_ref[...].astype(F32) * y).astype(BF16)
    y = jnp.dot(yb, wo_ref[...], preferred_element_type=F32)
    _tail(y, x_ref, mod_ref, g2_ref, rw_ref, rb_ref, x1_ref, h2_ref, ids_ref, wts_ref)


def _conv_tail(bg, cz, cw, wo, x, mod, g2, rw, rb):
    row = lambda i: (i, 0)
    tin, tout, tshape = _tail_specs()
    nb = TM // 16
    return pl.pallas_call(
        _conv_tail_kernel,
        grid=(S // TM,),
        in_specs=[pl.BlockSpec((TM, D), row),
                  pl.BlockSpec((TM, D), row),
                  pl.BlockSpec((16, D), lambda i: (jnp.maximum(i * nb - 1, 0), 0)),
                  pl.BlockSpec((16, D), lambda i: (jnp.minimum((i + 1) * nb, S // 16 - 1), 0)),
                  _const_spec((8, D)),
                  _const_spec((D, D))] + tin,
        out_specs=tout,
        out_shape=tshape,
        compiler_params=_cp(("arbitrary",)),
        name="conv_tail",
    )(bg, cz, cz, cz, cw, wo, x, mod, g2, rw, rb)


def _row_copy_kernel(sidx_ref, didx_ref, src_ref, *rest):
    dst_ref, sem = rest[-2], rest[-1]

    def row_dma(s, d):
        s = pl.multiple_of(s * SLABS, SLABS)
        d = pl.multiple_of(d * SLABS, SLABS)
        return pltpu.make_async_copy(src_ref.at[pl.ds(s, SLABS)], dst_ref.at[pl.ds(d, SLABS)], sem)

    def issue(j, carry):
        row_dma(sidx_ref[0, j], didx_ref[0, j]).start()
        return carry

    def drain(j, carry):
        row_dma(sidx_ref[0, j], didx_ref[0, j]).wait()
        return carry

    lax.fori_loop(0, COPY_CH, issue, 0, unroll=8)
    lax.fori_loop(0, COPY_CH, drain, 0, unroll=8)


def _row_copy(src, sidx, didx, n_dst, dst_init=None):
    n = sidx.shape[0]
    nch = n // COPY_CH
    smem = lambda: pl.BlockSpec((None, 1, COPY_CH), lambda i: (i, 0, 0), memory_space=pltpu.SMEM)
    extra = [] if dst_init is None else [dst_init]
    return pl.pallas_call(
        _row_copy_kernel,
        grid=(nch,),
        in_specs=[smem(), smem()] + [pl.BlockSpec(memory_space=pl.ANY)] * (1 + len(extra)),
        out_specs=pl.BlockSpec(memory_space=pl.ANY),
        out_shape=jax.ShapeDtypeStruct((n_dst * SLABS, 128), src.dtype),
        scratch_shapes=[pltpu.SemaphoreType.DMA(())],
        input_output_aliases={} if dst_init is None else {3: 0},
        compiler_params=pltpu.CompilerParams(dimension_semantics=("arbitrary",),
                                             has_side_effects=True),
        name="row_copy",
    )(sidx.reshape(nch, 1, COPY_CH), didx.reshape(nch, 1, COPY_CH), src, *extra)


def _moe_kernel(te_ref, first_ref, nused_ref, x_ref, wg_ref, wu_ref, wd_ref, o_ref,
                wg_s, wu_s, wd_s):
    i = pl.program_id(0)

    @pl.when(first_ref[i] == 1)
    def _():
        wg_s[...] = wg_ref[...].astype(BF16)
        wu_s[...] = wu_ref[...].astype(BF16)
        wd_s[...] = wd_ref[...].astype(BF16)

    @pl.when(i < nused_ref[0])
    def _():
        x = _load_rows(x_ref, MOE_TM).astype(BF16)
        g = jnp.dot(x, wg_s[...], preferred_element_type=F32)
        u = jnp.dot(x, wu_s[...], preferred_element_type=F32)
        a = (_silu(g) * u).astype(BF16)
        _store_rows(o_ref, jnp.dot(a, wd_s[...], preferred_element_type=F32), MOE_TM)

    @pl.when(i >= nused_ref[0])
    def _():
        o_ref[...] = jnp.zeros(o_ref.shape, F32)


def _moe(te, first, nused, xs, w_gate, w_up, w_down, layer):
    def xmap(i, te, first, nused):
        return (jnp.minimum(i, nused[0] - 1), 0)

    def omap(i, te, first, nused):
        return (i, 0)

    wmap = lambda i, te, first, nused: (layer, te[i], 0, 0)
    grid_spec = pltpu.PrefetchScalarGridSpec(
        num_scalar_prefetch=3,
        grid=(MOE_NT,),
        in_specs=[pl.BlockSpec((MOE_TM * SLABS, 128), xmap),
                  pl.BlockSpec((None, None, D, D_E), wmap),
                  pl.BlockSpec((None, None, D, D_E), wmap),
                  pl.BlockSpec((None, None, D_E, D), wmap)],
        out_specs=pl.BlockSpec((MOE_TM * SLABS, 128), omap),
        scratch_shapes=[pltpu.VMEM((D, D_E), BF16), pltpu.VMEM((D, D_E), BF16),
                        pltpu.VMEM((D_E, D), BF16)],
    )
    return pl.pallas_call(
        _moe_kernel,
        grid_spec=grid_spec,
        out_shape=jax.ShapeDtypeStruct((P_ROWS * SLABS, 128), F32),
        compiler_params=_cp(("arbitrary",)),
        name="moe",
    )(te, first, nused, xs, w_gate, w_up, w_down)


def _route_plan(ids):
    ef = ids[0:2].T.reshape(-1)
    oh = (ef[:, None] == jnp.arange(N_E, dtype=jnp.int32)[None, :]).astype(jnp.int32)
    csum = jnp.cumsum(oh, axis=0)
    rank = jnp.sum(csum * oh, axis=1) - 1
    cnt = csum[-1]
    pc = ((cnt + MOE_TM - 1) // MOE_TM) * MOE_TM
    end = jnp.cumsum(pc)
    off = end - pc
    pos = jnp.sum(oh * off[None, :], axis=1) + rank
    nused = (end[-1] // MOE_TM).astype(jnp.int32)
    ti = jnp.arange(MOE_NT, dtype=jnp.int32)
    te = jnp.sum((ti[:, None] >= (end // MOE_TM)[None, :]).astype(jnp.int32), axis=1)
    te_last = jnp.sum((nused - 1 >= end // MOE_TM).astype(jnp.int32))
    te = jnp.minimum(te, te_last).astype(jnp.int32)
    first = jnp.concatenate([jnp.ones((1,), jnp.int32), (te[1:] != te[:-1]).astype(jnp.int32)])
    return pos.astype(jnp.int32), te, first, nused.reshape(1)


def _moe_block(h2, ids, w_gate, w_up, w_down, layer):
    pos, te, first, nused = _route_plan(ids)
    tok = jnp.arange(2 * S, dtype=jnp.int32) // 2
    xs = _row_copy(h2, tok, pos, P_ROWS, jnp.zeros((P_ROWS * SLABS, 128), F32))
    ys = _moe(te, first, nused, xs, w_gate, w_up, w_down, layer)
    back = (jnp.arange(2 * S, dtype=jnp.int32) % 2) * S + tok
    return _row_copy(ys, pos, back, 2 * S)


def _combine(x1_ref, y0_ref, y1_ref, w_ref, mod_ref):
    w = w_ref[...]
    moe = w[:, 0:1] * _load_rows(y0_ref, TM) + w[:, 1:2] * _load_rows(y1_ref, TM)
    return x1_ref[...] + mod_ref[0, 0:1, :] * moe


def _fuse_next_kernel(x1_ref, y0_ref, y1_ref, w_ref, mod_ref, nmod_ref, g_ref, x2_ref, h_ref):
    x2 = _combine(x1_ref, y0_ref, y1_ref, w_ref, mod_ref)
    x2_ref[...] = x2
    nm = nmod_ref[0, 0:1, :]
    h_ref[...] = (_rms(x2) * g_ref[...] * (1.0 + nm[:, D:2 * D]) + nm[:, 0:D]).astype(BF16)


def _fuse_final_kernel(x1_ref, y0_ref, y1_ref, w_ref, mod_ref, g_ref, o_ref):
    x2 = _combine(x1_ref, y0_ref, y1_ref, w_ref, mod_ref)
    o_ref[...] = _rms(x2) * g_ref[...]


def _fuse_specs():
    row = lambda i: (i, 0)
    return [pl.BlockSpec((TM, D), row),
            pl.BlockSpec((TM * SLABS, 128), row),
            pl.BlockSpec((TM * SLABS, 128), lambda i: (i + S // TM, 0)),
            pl.BlockSpec((TM, 8), row)]


def _fuse_next(x1, yg, wt, mod_g2, nmod, g):
    row = lambda i: (i, 0)
    return pl.pallas_call(
        _fuse_next_kernel,
        grid=(S // TM,),
        in_specs=_fuse_specs() + [pl.BlockSpec((1, 8, D), lambda i: (0, 0, 0)),
                                  pl.BlockSpec((1, 8, 2 * D), lambda i: (0, 0, 0)),
                                  _const_spec((1, D))],
        out_specs=[pl.BlockSpec((TM, D), row), pl.BlockSpec((TM, D), row)],
        out_shape=[jax.ShapeDtypeStruct((S, D), F32), jax.ShapeDtypeStruct((S, D), BF16)],
        compiler_params=_cp(("arbitrary",)),
        name="fuse_next",
    )(x1, yg, yg, wt, mod_g2, nmod, g)


def _fuse_final(x1, yg, wt, mod_g2, g):
    row = lambda i: (i, 0)
    return pl.pallas_call(
        _fuse_final_kernel,
        grid=(S // TM,),
        in_specs=_fuse_specs() + [pl.BlockSpec((1, 8, D), lambda i: (0, 0, 0)),
                                  _const_spec((1, D))],
        out_specs=pl.BlockSpec((TM, D), row),
        out_shape=jax.ShapeDtypeStruct((S, D), F32),
        compiler_params=_cp(("arbitrary",)),
        name="fuse_final",
    )(x1, yg, yg, wt, mod_g2, g)


def _rope_tables():
    half = ROPE // 2
    row = jnp.repeat(jnp.arange(S // GRID_W), GRID_W).astype(F32)
    col = jnp.tile(jnp.arange(GRID_W), S // GRID_W).astype(F32)
    inv = THETA ** (-jnp.arange(0, half, 2, dtype=F32) / half)
    ang_r = row[:, None] * inv
    ang_c = col[:, None] * inv
    cr, sr, cc, sc = jnp.cos(ang_r), jnp.sin(ang_r), jnp.cos(ang_c), jnp.sin(ang_c)
    one = jnp.ones((S, 64), F32)
    zero = jnp.zeros((S, 64), F32)
    cos_t = jnp.concatenate([cr, cr, cc, cc, one], axis=1)
    sin_t = jnp.concatenate([-sr, sr, -sc, sc, zero], axis=1)
    cos_t = jnp.concatenate([cos_t, jnp.ones((CTX, 128), F32)], axis=0)
    sin_t = jnp.concatenate([sin_t, jnp.zeros((CTX, 128), F32)], axis=0)
    return cos_t, sin_t


def _prep_layer0(attn_in_w, q_up_w, kv_up_w):
    c0, c1, c2 = Q_LORA, Q_LORA + KV_LORA, Q_LORA + KV_LORA + ROPE
    w_in = jnp.concatenate([attn_in_w[:, :c2], jnp.zeros((D, 128 - ROPE), F32), attn_in_w[:, c2:]],
                           axis=1).astype(BF16)
    qw = q_up_w.reshape(Q_LORA, HEADS, NOPE + ROPE)
    wq = jnp.concatenate([qw, jnp.zeros((Q_LORA, HEADS, HQ - NOPE - ROPE), F32)],
                         axis=2).reshape(Q_LORA, HEADS * HQ).astype(BF16)
    kvw = kv_up_w.reshape(KV_LORA, HEADS, NOPE + VH)
    wk = kvw[:, :, :NOPE].reshape(KV_LORA, HEADS * NOPE).astype(BF16)
    wv = kvw[:, :, NOPE:].reshape(KV_LORA, HEADS * VH).astype(BF16)
    del c0, c1
    return w_in, wq, wk, wv


def _prep_router(router_w, router_b):
    perm = np.array([EPG * (s % N_G) + s // N_G for s in range(N_E)])
    rw = jnp.concatenate([router_w[:, perm], jnp.zeros((D, 128 - N_E), F32)], axis=1)
    rb = router_b[perm].reshape(N_E, 1)
    return rw, rb


def kernel(x, c, ctx, c_ctx, ada_w, ada_b, norm1_g, norm2_g, attn_in_w, q_norm_g, q_up_w, kv_norm_g,
           kv_up_w, conv_in_w, conv_w, mix_out_w, router_w, router_b, moe_w_gate, moe_w_up,
           moe_w_down, final_norm_g):
    x = x.reshape(S, D)
    mods = _ada(c.reshape(D), c_ctx, ada_w, ada_b)
    rw, rb = _prep_router(router_w, router_b)
    m1, gtab, dft = _fft_tables()
    cos_t, sin_t = _rope_tables()

    w_in, wq, wk, wv = _prep_layer0(attn_in_w[0], q_up_w[0], kv_up_w[0])
    xc = jnp.concatenate([x, ctx.reshape(CTX, D)], axis=0)
    qp, kp, v, z = _inproj(xc, mods[0:1, :, 0:2 * D], norm1_g[0:1], w_in, q_norm_g[0:1], wq,
                           kv_norm_g[0:1], wk, wv, cos_t, sin_t, dft)
    attn = _attention(qp, kp, v)
    four = _fourier_tokens(z, m1, gtab)
    wo = mix_out_w[0].astype(BF16)
    x1, h2, ids, wts = _mix_tail(attn, four, wo[:HEADS * VH], wo[HEADS * VH:], x,
                                 mods[0:1, :, 2 * D:5 * D], norm2_g[0:1], rw, rb)
    yg = _moe_block(h2, ids, moe_w_gate, moe_w_up, moe_w_down, 0)
    x2, h = _fuse_next(x1, yg, wts.T, mods[0:1, :, 5 * D:6 * D], mods[1:2, :, 0:2 * D], norm1_g[1:2])

    bg, cz = _conv_in(h, conv_in_w[0].astype(BF16))
    cw = jnp.concatenate([conv_w[0], jnp.zeros((5, D), F32)], axis=0)
    x1, h2, ids, wts = _conv_tail(bg, cz, cw, mix_out_w[1].astype(BF16), x2,
                                  mods[1:2, :, 2 * D:5 * D], norm2_g[1:2], rw, rb)
    yg = _moe_block(h2, ids, moe_w_gate, moe_w_up, moe_w_down, 1)
    out = _fuse_final(x1, yg, wts.T, mods[1:2, :, 5 * D:6 * D], final_norm_g.reshape(1, D))
    return out.reshape(1, S, D)
```

```python
import functools

import numpy as np
import jax
import jax.numpy as jnp
from jax import lax
from jax.experimental import pallas as pl
from jax.experimental.pallas import tpu as pltpu

F32 = jnp.float32
BF16 = jnp.bfloat16

D = 2048
S = 8192
CTX = 256
TT = S + CTX
GRID_W = 64
HEADS = 8
Q_LORA = 512
KV_LORA = 512
NOPE = 128
ROPE = 64
VH = 128
HQ = 256
THETA = 10000.0
SCALE = (NOPE + ROPE) ** -0.5
FW = 1024
FG = 128
N_E = 32
N_G = 8
EPG = 4
D_E = 512
EPS = 1e-6
NMOD = 6

TM = 256
TQ = 256
MOE_TM = 128
P_ROWS = 2 * S + N_E * MOE_TM
MOE_NT = P_ROWS // MOE_TM
FFT_N1 = 128
FFT_N2 = 64

VMEM_LIMIT = 56 * 1024 * 1024


def _cp(sem, limit=VMEM_LIMIT):
    return pltpu.CompilerParams(dimension_semantics=sem, vmem_limit_bytes=limit)


def _const_spec(shape):
    n = len(shape)
    return pl.BlockSpec(shape, lambda *a: (0,) * n, pipeline_mode=pl.Buffered(1))


SLABS = D // 128


def _load_rows(ref, n):
    return jnp.concatenate([ref[pl.ds(j, n, stride=SLABS), :] for j in range(SLABS)], axis=1)


def _store_rows(ref, val, n):
    for j in range(SLABS):
        ref[pl.ds(j, n, stride=SLABS), :] = val[:, j * 128:(j + 1) * 128]


def _rms(x):
    return x * lax.rsqrt(jnp.mean(x * x, axis=-1, keepdims=True) + EPS)


def _silu(x):
    return x * jax.nn.sigmoid(x)


ADA_BN = 1024


def _ada_kernel(cb_ref, w_ref, b_ref, o_ref):
    a0 = _silu(cb_ref[0])
    a1 = _silu(cb_ref[1])
    rows = []
    for j in range(ADA_BN // 128):
        w = w_ref[0, :, j * 128:(j + 1) * 128]
        r0 = jnp.sum(w * a0, axis=0, keepdims=True)
        r1 = jnp.sum(w * a1, axis=0, keepdims=True)
        rows.append(jnp.concatenate([r0, r1, jnp.zeros((6, 128), F32)], axis=0))
    o_ref[0] = jnp.concatenate(rows, axis=1) + b_ref[0]


def _ada(c, c_ctx, ada_w, ada_b):
    depth = ada_w.shape[0]
    n = ada_w.shape[2]
    cb = jnp.stack([jnp.broadcast_to(c.reshape(D, 1), (D, 128)),
                    jnp.broadcast_to(c_ctx.reshape(D, 1), (D, 128))])
    return pl.pallas_call(
        _ada_kernel,
        grid=(depth, n // ADA_BN),
        in_specs=[pl.BlockSpec((2, D, 128), lambda l, j: (0, 0, 0)),
                  pl.BlockSpec((1, D, ADA_BN), lambda l, j: (l, 0, j)),
                  pl.BlockSpec((1, 1, ADA_BN), lambda l, j: (l, 0, j))],
        out_specs=pl.BlockSpec((1, 8, ADA_BN), lambda l, j: (l, 0, j)),
        out_shape=jax.ShapeDtypeStruct((depth, 8, n), F32),
        compiler_params=_cp(("arbitrary", "arbitrary")),
        name="ada",
    )(cb, ada_w, ada_b.reshape(depth, 1, n))


W_IN_COLS = Q_LORA + KV_LORA + 128 + FW


def _rope_chunk(x, cos, sin):
    lane = lax.broadcasted_iota(jnp.int32, x.shape, 1)
    sw = jnp.where((lane % 32) < 16, pltpu.roll(x, 112, 1), pltpu.roll(x, 16, 1))
    return x * cos + sw * sin


def _inproj_kernel(x_ref, mod_ref, g_ref, win_ref, qg_ref, wq_ref, kg_ref, wk_ref, wv_ref,
                   cos_ref, sin_ref, dft_ref, q_out, k_out, v_out, z_out):
    i = pl.program_id(0)
    is_ctx = i >= S // TM
    mod = jnp.where(is_ctx, mod_ref[0, 1:2, :], mod_ref[0, 0:1, :])
    h = _rms(x_ref[...]) * g_ref[...] * (1.0 + mod[:, D:2 * D]) + mod[:, 0:D]
    u = jnp.dot(h.astype(BF16), win_ref[...], preferred_element_type=F32)
    cos = cos_ref[...]
    sin = sin_ref[...]
    qc = (_rms(u[:, 0:Q_LORA]) * qg_ref[...]).astype(BF16)
    q = jnp.dot(qc, wq_ref[...], preferred_element_type=F32)
    for hd in range(HEADS):
        q_out[:, hd * HQ:hd * HQ + NOPE] = (q[:, hd * HQ:hd * HQ + NOPE] * SCALE).astype(BF16)
        qr = _rope_chunk(q[:, hd * HQ + NOPE:(hd + 1) * HQ], cos, sin) * SCALE
        q_out[:, hd * HQ + NOPE:(hd + 1) * HQ] = qr.astype(BF16)
    kc = (_rms(u[:, Q_LORA:Q_LORA + KV_LORA]) * kg_ref[...]).astype(BF16)
    kn = jnp.dot(kc, wk_ref[...], preferred_element_type=F32)
    v_out[...] = jnp.dot(kc, wv_ref[...], preferred_element_type=F32).astype(BF16)
    kr = _rope_chunk(u[:, 2 * Q_LORA:2 * Q_LORA + 128], cos, sin).astype(BF16)
    for hd in range(HEADS):
        k_out[:, hd * HQ:hd * HQ + NOPE] = kn[:, hd * NOPE:(hd + 1) * NOPE].astype(BF16)
        k_out[:, hd * HQ + NOPE:(hd + 1) * HQ] = kr
    f0 = 2 * Q_LORA + 128
    for g in range(FW // FG):
        fg = u[:, f0 + g * FG:f0 + (g + 1) * FG].astype(BF16)
        zz = jnp.dot(fg, dft_ref[...], preferred_element_type=F32)
        z_out[0, :, g * FG:(g + 1) * FG] = zz[:, 0:FG].astype(BF16)
        z_out[1, :, g * FG:(g + 1) * FG] = zz[:, FG:2 * FG].astype(BF16)


def _inproj(xc, mods, g1, w_in, qg, wq, kg, wk, wv, cos_t, sin_t, dft):
    nt = TT // TM
    row = lambda i: (i, 0)
    return pl.pallas_call(
        _inproj_kernel,
        grid=(nt,),
        in_specs=[pl.BlockSpec((TM, D), row),
                  pl.BlockSpec((1, 8, 2 * D), lambda i: (0, 0, 0)),
                  _const_spec((1, D)),
                  _const_spec((D, W_IN_COLS)),
                  _const_spec((1, Q_LORA)),
                  _const_spec((Q_LORA, HEADS * HQ)),
                  _const_spec((1, KV_LORA)),
                  _const_spec((KV_LORA, HEADS * NOPE)),
                  _const_spec((KV_LORA, HEADS * VH)),
                  pl.BlockSpec((TM, 128), row),
                  pl.BlockSpec((TM, 128), row),
                  _const_spec((FG, 2 * FG))],
        out_specs=[pl.BlockSpec((TM, HEADS * HQ), row),
                   pl.BlockSpec((TM, HEADS * HQ), row),
                   pl.BlockSpec((TM, HEADS * VH), row),
                   pl.BlockSpec((2, TM, FW), lambda i: (0, i, 0))],
        out_shape=[jax.ShapeDtypeStruct((TT, HEADS * HQ), BF16),
                   jax.ShapeDtypeStruct((TT, HEADS * HQ), BF16),
                   jax.ShapeDtypeStruct((TT, HEADS * VH), BF16),
                   jax.ShapeDtypeStruct((2, TT, FW), BF16)],
        compiler_params=_cp(("arbitrary",)),
        name="inproj",
    )(xc, mods, g1, w_in, qg, wq, kg, wk, wv, cos_t, sin_t, dft)


def _attn_kernel(q_ref, k_ref, v_ref, o_ref):
    s = lax.dot_general(q_ref[...], k_ref[...], (((1,), (1,)), ((), ())),
                        preferred_element_type=F32)
    m = jnp.max(s, axis=-1, keepdims=True)
    p = jnp.exp(s - m)
    l = jnp.sum(p, axis=-1, keepdims=True)
    o = jnp.dot(p.astype(BF16), v_ref[...], preferred_element_type=F32)
    o_ref[...] = (o / l).astype(BF16)


def _attention(qp, kp, v):
    return pl.pallas_call(
        _attn_kernel,
        grid=(HEADS, S // TQ),
        in_specs=[pl.BlockSpec((TQ, HQ), lambda h, i: (i, h)),
                  pl.BlockSpec((TT, HQ), lambda h, i: (0, h)),
                  pl.BlockSpec((TT, VH), lambda h, i: (0, h))],
        out_specs=pl.BlockSpec((TQ, VH), lambda h, i: (i, h)),
        out_shape=jax.ShapeDtypeStruct((S, HEADS * VH), BF16),
        compiler_params=_cp(("arbitrary", "arbitrary")),
        name="attention",
    )(qp, kp, v)


FFT1_BN = 2048


def _fft1_kernel(m_ref, z_ref, y_ref):
    y_ref[...] = jnp.dot(m_ref[...], z_ref[...], preferred_element_type=F32).astype(BF16)


def _fft2_kernel(g_ref, y_ref, o_ref):
    y = y_ref[...].reshape(2 * FFT_N2, FW)
    o_ref[0] = jnp.dot(g_ref[0], y, preferred_element_type=F32).astype(BF16)


def _fft_tables():
    k1 = np.arange(FFT_N1, dtype=np.float64)
    a = 2.0 * np.pi * np.outer(k1, k1) / FFT_N1
    c1, s1 = np.cos(a), np.sin(a)
    m1 = np.block([[c1, s1], [-s1, c1]])
    t2 = np.arange(FFT_N2, dtype=np.float64)
    k2 = np.arange(FFT_N2, dtype=np.float64)
    th = 2.0 * np.pi * (k1[:, None, None] * t2[None, None, :] / (FFT_N1 * FFT_N2)
                        + k2[None, :, None] * t2[None, None, :] / FFT_N2)
    g = np.concatenate([np.cos(th), np.sin(th)], axis=2) / np.sqrt(float(S * FG))
    cc = np.arange(FG, dtype=np.float64)
    ac = 2.0 * np.pi * np.outer(cc, cc) / FG
    dft = np.concatenate([np.cos(ac), -np.sin(ac)], axis=1)
    return (jnp.asarray(m1, F32).astype(BF16), jnp.asarray(g, F32).astype(BF16),
            jnp.asarray(dft, F32).astype(BF16))


def _fourier_tokens(z, m1, gtab):
    ncol = FFT_N2 * FW
    z2 = z[:, :S].reshape(2 * FFT_N1, ncol)
    y = pl.pallas_call(
        _fft1_kernel,
        grid=(ncol // FFT1_BN,),
        in_specs=[_const_spec((2 * FFT_N1, 2 * FFT_N1)),
                  pl.BlockSpec((2 * FFT_N1, FFT1_BN), lambda j: (0, j))],
        out_specs=pl.BlockSpec((2 * FFT_N1, FFT1_BN), lambda j: (0, j)),
        out_shape=jax.ShapeDtypeStruct((2 * FFT_N1, ncol), BF16),
        compiler_params=_cp(("arbitrary",)),
        name="fft1",
    )(m1, z2)
    y4 = y.reshape(2, FFT_N1, FFT_N2, FW)
    o = pl.pallas_call(
        _fft2_kernel,
        grid=(FFT_N1,),
        in_specs=[pl.BlockSpec((1, FFT_N2, 2 * FFT_N2), lambda k: (k, 0, 0)),
                  pl.BlockSpec((2, None, FFT_N2, FW), lambda k: (0, k, 0, 0))],
        out_specs=pl.BlockSpec((1, FFT_N2, FW), lambda k: (k, 0, 0)),
        out_shape=jax.ShapeDtypeStruct((FFT_N1, FFT_N2, FW), BF16),
        compiler_params=_cp(("arbitrary",)),
        name="fft2",
    )(gtab, y4)
    return jnp.transpose(o, (1, 0, 2)).reshape(S, FW)


def _route(h2, rw_ref, rb_ref, ids_ref, wts_ref):
    logits = jnp.dot(h2, rw_ref[...], precision=lax.Precision.HIGHEST,
                     preferred_element_type=F32)
    lt = jnp.transpose(logits)[0:N_E, :]
    sc = jax.nn.sigmoid(lt)
    bi = sc + rb_ref[...]
    sj = [sc[j * N_G:(j + 1) * N_G] for j in range(EPG)]
    bj = [bi[j * N_G:(j + 1) * N_G] for j in range(EPG)]
    m01, n01 = jnp.maximum(bj[0], bj[1]), jnp.minimum(bj[0], bj[1])
    m23, n23 = jnp.maximum(bj[2], bj[3]), jnp.minimum(bj[2], bj[3])
    gs = jnp.maximum(m01, m23) + jnp.maximum(jnp.minimum(m01, m23), jnp.maximum(n01, n23))
    gmax = jnp.max(gs, axis=0, keepdims=True)
    gi = lax.broadcasted_iota(jnp.int32, gs.shape, 0)
    best = jnp.min(jnp.where(gs == gmax, gi, N_G), axis=0, keepdims=True)
    sel = gi == best
    b = [jnp.sum(jnp.where(sel, x, 0.0), axis=0, keepdims=True) for x in bj]
    s = [jnp.sum(jnp.where(sel, x, 0.0), axis=0, keepdims=True) for x in sj]
    b1 = jnp.maximum(jnp.maximum(b[0], b[1]), jnp.maximum(b[2], b[3]))
    j1 = jnp.full(b1.shape, EPG, jnp.int32)
    for j in reversed(range(EPG)):
        j1 = jnp.where(b[j] == b1, j, j1)
    c = [jnp.where(j1 == j, -jnp.inf, b[j]) for j in range(EPG)]
    b2 = jnp.maximum(jnp.maximum(c[0], c[1]), jnp.maximum(c[2], c[3]))
    j2 = jnp.full(b1.shape, EPG, jnp.int32)
    for j in reversed(range(EPG)):
        j2 = jnp.where(c[j] == b2, j, j2)
    s1 = sum(jnp.where(j1 == j, s[j], 0.0) for j in range(EPG))
    s2 = sum(jnp.where(j2 == j, s[j], 0.0) for j in range(EPG))
    tot = s1 + s2
    zi = jnp.zeros((6,) + b1.shape[1:], jnp.int32)
    zf = jnp.zeros((6,) + b1.shape[1:], F32)
    ids_ref[...] = jnp.concatenate([best * EPG + j1, best * EPG + j2, zi], axis=0)
    wts_ref[...] = jnp.concatenate([s1 / tot, s2 / tot, zf], axis=0)


def _tail(y, x_ref, mod_ref, g2_ref, rw_ref, rb_ref, x1_ref, h2_ref, ids_ref, wts_ref):
    mod = mod_ref[0, 0:1, :]
    x1 = x_ref[...] + mod[:, 0:D] * y
    x1_ref[...] = x1
    h2 = _rms(x1) * g2_ref[...] * (1.0 + mod[:, 2 * D:3 * D]) + mod[:, D:2 * D]
    _store_rows(h2_ref, h2, TM)
    _route(h2, rw_ref, rb_ref, ids_ref, wts_ref)


def _mix_tail_kernel(a_ref, f_ref, wa_ref, wf_ref, x_ref, mod_ref, g2_ref, rw_ref, rb_ref,
                     x1_ref, h2_ref, ids_ref, wts_ref):
    y = (jnp.dot(a_ref[...], wa_ref[...], preferred_element_type=F32)
         + jnp.dot(f_ref[...], wf_ref[...], preferred_element_type=F32))
    _tail(y, x_ref, mod_ref, g2_ref, rw_ref, rb_ref, x1_ref, h2_ref, ids_ref, wts_ref)


def _tail_specs():
    row = lambda i: (i, 0)
    in_specs = [pl.BlockSpec((TM, D), row),
                pl.BlockSpec((1, 8, 3 * D), lambda i: (0, 0, 0)),
                _const_spec((1, D)),
                _const_spec((D, 128)),
                _const_spec((N_E, 1))]
    out_specs = [pl.BlockSpec((TM, D), row),
                 pl.BlockSpec((TM * SLABS, 128), row),
                 pl.BlockSpec((8, TM), lambda i: (0, i)),
                 pl.BlockSpec((8, TM), lambda i: (0, i))]
    out_shape = [jax.ShapeDtypeStruct((S, D), F32),
                 jax.ShapeDtypeStruct((S * SLABS, 128), F32),
                 jax.ShapeDtypeStruct((8, S), jnp.int32),
                 jax.ShapeDtypeStruct((8, S), F32)]
    return in_specs, out_specs, out_shape


def _mix_tail(attn, four, wa, wf, x, mod, g2, rw, rb):
    row = lambda i: (i, 0)
    tin, tout, tshape = _tail_specs()
    return pl.pallas_call(
        _mix_tail_kernel,
        grid=(S // TM,),
        in_specs=[pl.BlockSpec((TM, HEADS * VH), row),
                  pl.BlockSpec((TM, FW), row),
                  _const_spec((HEADS * VH, D)),
                  _const_spec((FW, D))] + tin,
        out_specs=tout,
        out_shape=tshape,
        compiler_params=_cp(("arbitrary",)),
        name="mix_tail",
    )(attn, four, wa, wf, x, mod, g2, rw, rb)


def _conv_in_kernel(h_ref, w_ref, b_ref, cz_ref):
    u = jnp.dot(h_ref[...], w_ref[...], preferred_element_type=F32)
    b_ref[...] = u[:, 0:D].astype(BF16)
    cz_ref[...] = (u[:, D:2 * D] * u[:, 2 * D:3 * D]).astype(BF16)


def _conv_in(h, w):
    row = lambda i: (i, 0)
    return pl.pallas_call(
        _conv_in_kernel,
        grid=(S // TM,),
        in_specs=[pl.BlockSpec((TM, D), row), _const_spec((D, 3 * D))],
        out_specs=[pl.BlockSpec((TM, D), row), pl.BlockSpec((TM, D), row)],
        out_shape=[jax.ShapeDtypeStruct((S, D), BF16), jax.ShapeDtypeStruct((S, D), BF16)],
        compiler_params=_cp(("arbitrary",)),
        name="conv_in",
    )(h, w)


def _conv_tail_kernel(b_ref, cz_ref, prev_ref, next_ref, cw_ref, wo_ref, x_ref, mod_ref, g2_ref,
                      rw_ref, rb_ref, x1_ref, h2_ref, ids_ref, wts_ref):
    i = pl.program_id(0)
    cz = cz_ref[...].astype(F32)
    rid = lax.broadcasted_iota(jnp.int32, cz.shape, 0)
    prev_row = jnp.where(i > 0, prev_ref[15:16, :].astype(F32), 0.0)
    next_row = jnp.where(i < S // TM - 1, next_ref[0:1, :].astype(F32), 0.0)
    dn = jnp.where(rid == 0, prev_row, pltpu.roll(cz, 1, 0))
    up = jnp.where(rid == TM - 1, next_row, pltpu.roll(cz, TM - 1, 0))
    y = dn * cw_ref[0:1, :] + cz * cw_ref[1:2, :] + up * cw_ref[2:3, :]
    yb = (b_ref[...].astype(F32) * y).astype(BF16)
    y = jnp.dot(yb, wo_ref[...], preferred_element_type=F32)
    _tail(y, x_ref, mod_ref, g2_ref, rw_ref, rb_ref, x1_ref, h2_ref, ids_ref, wts_ref)


def _conv_tail(bg, cz, cw, wo, x, mod, g2, rw, rb):
    row = lambda i: (i, 0)
    tin, tout, tshape = _tail_specs()
    nb = TM // 16
    return pl.pallas_call(
        _conv_tail_kernel,
        grid=(S // TM,),
        in_specs=[pl.BlockSpec((TM, D), row),
                  pl.BlockSpec((TM, D), row),
                  pl.BlockSpec((16, D), lambda i: (jnp.maximum(i * nb - 1, 0), 0)),
                  pl.BlockSpec((16, D), lambda i: (jnp.minimum((i + 1) * nb, S // 16 - 1), 0)),
                  _const_spec((8, D)),
                  _const_spec((D, D))] + tin,
        out_specs=tout,
        out_shape=tshape,
        compiler_params=_cp(("arbitrary",)),
        name="conv_tail",
    )(bg, cz, cz, cz, cw, wo, x, mod, g2, rw, rb)


def _row_slab(ref, r):
    return ref.at[pl.ds(pl.multiple_of(r * SLABS, SLABS), SLABS)]


def _scatter_kernel(pos_ref, h_ref, init_ref, xs_ref, sem):
    del init_ref

    def dma(j):
        return pltpu.make_async_copy(_row_slab(h_ref, j // 2), _row_slab(xs_ref, pos_ref[0, j]), sem)

    def issue(j, carry):
        dma(j).start()
        return carry

    def drain(j, carry):
        dma(j).wait()
        return carry

    lax.fori_loop(0, 2 * TM, issue, 0, unroll=8)
    lax.fori_loop(0, 2 * TM, drain, 0, unroll=8)


def _scatter_rows(h2, pos):
    nt = S // TM
    return pl.pallas_call(
        _scatter_kernel,
        grid=(nt,),
        in_specs=[pl.BlockSpec((None, 1, 2 * TM), lambda i: (i, 0, 0), memory_space=pltpu.SMEM),
                  pl.BlockSpec((TM * SLABS, 128), lambda i: (i, 0)),
                  pl.BlockSpec(memory_space=pl.ANY)],
        out_specs=pl.BlockSpec(memory_space=pl.ANY),
        out_shape=jax.ShapeDtypeStruct((P_ROWS * SLABS, 128), F32),
        scratch_shapes=[pltpu.SemaphoreType.DMA(())],
        input_output_aliases={2: 0},
        compiler_params=pltpu.CompilerParams(dimension_semantics=("arbitrary",),
                                             has_side_effects=True),
        name="scatter_rows",
    )(pos.reshape(nt, 1, 2 * TM), h2, jnp.zeros((P_ROWS * SLABS, 128), F32))


def _moe_kernel(te_ref, first_ref, nused_ref, x_ref, wg_ref, wu_ref, wd_ref, o_ref,
                wg_s, wu_s, wd_s):
    i = pl.program_id(0)

    @pl.when(first_ref[i] == 1)
    def _():
        wg_s[...] = wg_ref[...].astype(BF16)
        wu_s[...] = wu_ref[...].astype(BF16)
        wd_s[...] = wd_ref[...].astype(BF16)

    @pl.when(i < nused_ref[0])
    def _():
        x = _load_rows(x_ref, MOE_TM).astype(BF16)
        g = jnp.dot(x, wg_s[...], preferred_element_type=F32)
        u = jnp.dot(x, wu_s[...], preferred_element_type=F32)
        a = (_silu(g) * u).astype(BF16)
        _store_rows(o_ref, jnp.dot(a, wd_s[...], preferred_element_type=F32), MOE_TM)

    @pl.when(i >= nused_ref[0])
    def _():
        o_ref[...] = jnp.zeros(o_ref.shape, F32)


def _moe(te, first, nused, xs, w_gate, w_up, w_down, layer):
    def xmap(i, te, first, nused):
        return (jnp.minimum(i, nused[0] - 1), 0)

    def omap(i, te, first, nused):
        return (i, 0)

    wmap = lambda i, te, first, nused: (layer, te[i], 0, 0)
    grid_spec = pltpu.PrefetchScalarGridSpec(
        num_scalar_prefetch=3,
        grid=(MOE_NT,),
        in_specs=[pl.BlockSpec((MOE_TM * SLABS, 128), xmap),
                  pl.BlockSpec((None, None, D, D_E), wmap),
                  pl.BlockSpec((None, None, D, D_E), wmap),
                  pl.BlockSpec((None, None, D_E, D), wmap)],
        out_specs=pl.BlockSpec((MOE_TM * SLABS, 128), omap),
        scratch_shapes=[pltpu.VMEM((D, D_E), BF16), pltpu.VMEM((D, D_E), BF16),
                        pltpu.VMEM((D_E, D), BF16)],
    )
    return pl.pallas_call(
        _moe_kernel,
        grid_spec=grid_spec,
        out_shape=jax.ShapeDtypeStruct((P_ROWS * SLABS, 128), F32),
        compiler_params=_cp(("arbitrary",)),
        name="moe",
    )(te, first, nused, xs, w_gate, w_up, w_down)


def _route_plan(ids):
    ef = ids[0:2].T.reshape(-1)
    oh = (ef[:, None] == jnp.arange(N_E, dtype=jnp.int32)[None, :]).astype(jnp.int32)
    csum = jnp.cumsum(oh, axis=0)
    rank = jnp.sum(csum * oh, axis=1) - 1
    cnt = csum[-1]
    pc = ((cnt + MOE_TM - 1) // MOE_TM) * MOE_TM
    end = jnp.cumsum(pc)
    off = end - pc
    pos = jnp.sum(oh * off[None, :], axis=1) + rank
    nused = (end[-1] // MOE_TM).astype(jnp.int32)
    ti = jnp.arange(MOE_NT, dtype=jnp.int32)
    te = jnp.sum((ti[:, None] >= (end // MOE_TM)[None, :]).astype(jnp.int32), axis=1)
    te_last = jnp.sum((nused - 1 >= end // MOE_TM).astype(jnp.int32))
    te = jnp.minimum(te, te_last).astype(jnp.int32)
    first = jnp.concatenate([jnp.ones((1,), jnp.int32), (te[1:] != te[:-1]).astype(jnp.int32)])
    return pos.astype(jnp.int32), te, first, nused.reshape(1)


def _moe_block(h2, ids, w_gate, w_up, w_down, layer):
    pos, te, first, nused = _route_plan(ids)
    xs = _scatter_rows(h2, pos)
    ys = _moe(te, first, nused, xs, w_gate, w_up, w_down, layer)
    return ys, pos.reshape(S // TM, 1, 2 * TM)


def _combine(pos_ref, ys_ref, x1_ref, w_ref, mod_ref, y0_buf, y1_buf, sem):
    def dma(r, k, buf):
        return pltpu.make_async_copy(_row_slab(ys_ref, pos_ref[0, 2 * r + k]), _row_slab(buf, r), sem)

    def issue(r, carry):
        dma(r, 0, y0_buf).start()
        dma(r, 1, y1_buf).start()
        return carry

    def drain(r, carry):
        dma(r, 0, y0_buf).wait()
        dma(r, 1, y1_buf).wait()
        return carry

    lax.fori_loop(0, TM, issue, 0, unroll=8)
    lax.fori_loop(0, TM, drain, 0, unroll=8)
    w = w_ref[...]
    moe = w[:, 0:1] * _load_rows(y0_buf, TM) + w[:, 1:2] * _load_rows(y1_buf, TM)
    return x1_ref[...] + mod_ref[0, 0:1, :] * moe


def _fuse_next_kernel(pos_ref, ys_ref, x1_ref, w_ref, mod_ref, nmod_ref, g_ref, x2_ref, h_ref,
                      y0_buf, y1_buf, sem):
    x2 = _combine(pos_ref, ys_ref, x1_ref, w_ref, mod_ref, y0_buf, y1_buf, sem)
    x2_ref[...] = x2
    nm = nmod_ref[0, 0:1, :]
    h_ref[...] = (_rms(x2) * g_ref[...] * (1.0 + nm[:, D:2 * D]) + nm[:, 0:D]).astype(BF16)


def _fuse_final_kernel(pos_ref, ys_ref, x1_ref, w_ref, mod_ref, g_ref, o_ref, y0_buf, y1_buf, sem):
    x2 = _combine(pos_ref, ys_ref, x1_ref, w_ref, mod_ref, y0_buf, y1_buf, sem)
    o_ref[...] = _rms(x2) * g_ref[...]


def _fuse_specs():
    row = lambda i: (i, 0)
    return [pl.BlockSpec((None, 1, 2 * TM), lambda i: (i, 0, 0), memory_space=pltpu.SMEM),
            pl.BlockSpec(memory_space=pl.ANY),
            pl.BlockSpec((TM, D), row),
            pl.BlockSpec((TM, 8), row),
            pl.BlockSpec((1, 8, D), lambda i: (0, 0, 0))]


def _fuse_scratch():
    return [pltpu.VMEM((TM * SLABS, 128), F32), pltpu.VMEM((TM * SLABS, 128), F32),
            pltpu.SemaphoreType.DMA(())]


def _fuse_next(pos, ys, x1, wt, mod_g2, nmod, g):
    row = lambda i: (i, 0)
    return pl.pallas_call(
        _fuse_next_kernel,
        grid=(S // TM,),
        in_specs=_fuse_specs() + [pl.BlockSpec((1, 8, 2 * D), lambda i: (0, 0, 0)),
                                  _const_spec((1, D))],
        out_specs=[pl.BlockSpec((TM, D), row), pl.BlockSpec((TM, D), row)],
        out_shape=[jax.ShapeDtypeStruct((S, D), F32), jax.ShapeDtypeStruct((S, D), BF16)],
        scratch_shapes=_fuse_scratch(),
        compiler_params=_cp(("arbitrary",)),
        name="fuse_next",
    )(pos, ys, x1, wt, mod_g2, nmod, g)


def _fuse_final(pos, ys, x1, wt, mod_g2, g):
    row = lambda i: (i, 0)
    return pl.pallas_call(
        _fuse_final_kernel,
        grid=(S // TM,),
        in_specs=_fuse_specs() + [_const_spec((1, D))],
        out_specs=pl.BlockSpec((TM, D), row),
        out_shape=jax.ShapeDtypeStruct((S, D), F32),
        scratch_shapes=_fuse_scratch(),
        compiler_params=_cp(("arbitrary",)),
        name="fuse_final",
    )(pos, ys, x1, wt, mod_g2, g)


def _rope_tables():
    half = ROPE // 2
    row = jnp.repeat(jnp.arange(S // GRID_W), GRID_W).astype(F32)
    col = jnp.tile(jnp.arange(GRID_W), S // GRID_W).astype(F32)
    inv = THETA ** (-jnp.arange(0, half, 2, dtype=F32) / half)
    ang_r = row[:, None] * inv
    ang_c = col[:, None] * inv
    cr, sr, cc, sc = jnp.cos(ang_r), jnp.sin(ang_r), jnp.cos(ang_c), jnp.sin(ang_c)
    one = jnp.ones((S, 64), F32)
    zero = jnp.zeros((S, 64), F32)
    cos_t = jnp.concatenate([cr, cr, cc, cc, one], axis=1)
    sin_t = jnp.concatenate([-sr, sr, -sc, sc, zero], axis=1)
    cos_t = jnp.concatenate([cos_t, jnp.ones((CTX, 128), F32)], axis=0)
    sin_t = jnp.concatenate([sin_t, jnp.zeros((CTX, 128), F32)], axis=0)
    return cos_t, sin_t


def _prep_layer0(attn_in_w, q_up_w, kv_up_w):
    c0, c1, c2 = Q_LORA, Q_LORA + KV_LORA, Q_LORA + KV_LORA + ROPE
    w_in = jnp.concatenate([attn_in_w[:, :c2], jnp.zeros((D, 128 - ROPE), F32), attn_in_w[:, c2:]],
                           axis=1).astype(BF16)
    qw = q_up_w.reshape(Q_LORA, HEADS, NOPE + ROPE)
    wq = jnp.concatenate([qw, jnp.zeros((Q_LORA, HEADS, HQ - NOPE - ROPE), F32)],
                         axis=2).reshape(Q_LORA, HEADS * HQ).astype(BF16)
    kvw = kv_up_w.reshape(KV_LORA, HEADS, NOPE + VH)
    wk = kvw[:, :, :NOPE].reshape(KV_LORA, HEADS * NOPE).astype(BF16)
    wv = kvw[:, :, NOPE:].reshape(KV_LORA, HEADS * VH).astype(BF16)
    del c0, c1
    return w_in, wq, wk, wv


def _prep_router(router_w, router_b):
    perm = np.array([EPG * (s % N_G) + s // N_G for s in range(N_E)])
    rw = jnp.concatenate([router_w[:, perm], jnp.zeros((D, 128 - N_E), F32)], axis=1)
    rb = router_b[perm].reshape(N_E, 1)
    return rw, rb


def kernel(x, c, ctx, c_ctx, ada_w, ada_b, norm1_g, norm2_g, attn_in_w, q_norm_g, q_up_w, kv_norm_g,
           kv_up_w, conv_in_w, conv_w, mix_out_w, router_w, router_b, moe_w_gate, moe_w_up,
           moe_w_down, final_norm_g):
    x = x.reshape(S, D)
    mods = _ada(c.reshape(D), c_ctx, ada_w, ada_b)
    rw, rb = _prep_router(router_w, router_b)
    m1, gtab, dft = _fft_tables()
    cos_t, sin_t = _rope_tables()

    w_in, wq, wk, wv = _prep_layer0(attn_in_w[0], q_up_w[0], kv_up_w[0])
    xc = jnp.concatenate([x, ctx.reshape(CTX, D)], axis=0)
    qp, kp, v, z = _inproj(xc, mods[0:1, :, 0:2 * D], norm1_g[0:1], w_in, q_norm_g[0:1], wq,
                           kv_norm_g[0:1], wk, wv, cos_t, sin_t, dft)
    attn = _attention(qp, kp, v)
    four = _fourier_tokens(z, m1, gtab)
    wo = mix_out_w[0].astype(BF16)
    x1, h2, ids, wts = _mix_tail(attn, four, wo[:HEADS * VH], wo[HEADS * VH:], x,
                                 mods[0:1, :, 2 * D:5 * D], norm2_g[0:1], rw, rb)
    ys, pos = _moe_block(h2, ids, moe_w_gate, moe_w_up, moe_w_down, 0)
    x2, h = _fuse_next(pos, ys, x1, wts.T, mods[0:1, :, 5 * D:6 * D], mods[1:2, :, 0:2 * D],
                       norm1_g[1:2])

    bg, cz = _conv_in(h, conv_in_w[0].astype(BF16))
    cw = jnp.concatenate([conv_w[0], jnp.zeros((5, D), F32)], axis=0)
    x1, h2, ids, wts = _conv_tail(bg, cz, cw, mix_out_w[1].astype(BF16), x2,
                                  mods[1:2, :, 2 * D:5 * D], norm2_g[1:2], rw, rb)
    ys, pos = _moe_block(h2, ids, moe_w_gate, moe_w_up, moe_w_down, 1)
    out = _fuse_final(pos, ys, x1, wts.T, mods[1:2, :, 5 * D:6 * D], final_norm_g.reshape(1, D))
    return out.reshape(1, S, D)
```

```python
import functools

import numpy as np
import jax
import jax.numpy as jnp
from jax import lax
from jax.experimental import pallas as pl
from jax.experimental.pallas import tpu as pltpu

F32 = jnp.float32
BF16 = jnp.bfloat16

D = 2048
S = 8192
CTX = 256
TT = S + CTX
GRID_W = 64
HEADS = 8
Q_LORA = 512
KV_LORA = 512
NOPE = 128
ROPE = 64
VH = 128
HQ = 256
THETA = 10000.0
QSCALE = (NOPE + ROPE) ** -0.5 * float(np.log2(np.e))
FW = 1024
FG = 128
N_E = 32
N_G = 8
EPG = 4
D_E = 512
EPS = 1e-6
NMOD = 6

TM = 256
TQ = 256
MOE_TM = 128
P_ROWS = 2 * S + N_E * MOE_TM
MOE_NT = P_ROWS // MOE_TM
FFT_N1 = 128
FFT_N2 = 64

VMEM_LIMIT = 56 * 1024 * 1024


def _cp(sem, limit=VMEM_LIMIT):
    return pltpu.CompilerParams(dimension_semantics=sem, vmem_limit_bytes=limit)


def _const_spec(shape):
    n = len(shape)
    return pl.BlockSpec(shape, lambda *a: (0,) * n, pipeline_mode=pl.Buffered(1))


SLABS = D // 128


def _load_rows(ref, n):
    return jnp.concatenate([ref[pl.ds(j, n, stride=SLABS), :] for j in range(SLABS)], axis=1)


def _store_rows(ref, val, n):
    for j in range(SLABS):
        ref[pl.ds(j, n, stride=SLABS), :] = val[:, j * 128:(j + 1) * 128]


def _rms(x):
    return x * lax.rsqrt(jnp.mean(x * x, axis=-1, keepdims=True) + EPS)


def _silu(x):
    return x * jax.nn.sigmoid(x)


ADA_BN = 1024


def _ada_kernel(cb_ref, w_ref, b_ref, o_ref):
    a0 = _silu(cb_ref[0])
    a1 = _silu(cb_ref[1])
    rows = []
    for j in range(ADA_BN // 128):
        w = w_ref[0, :, j * 128:(j + 1) * 128]
        r0 = jnp.sum(w * a0, axis=0, keepdims=True)
        r1 = jnp.sum(w * a1, axis=0, keepdims=True)
        rows.append(jnp.concatenate([r0, r1, jnp.zeros((6, 128), F32)], axis=0))
    o_ref[0] = jnp.concatenate(rows, axis=1) + b_ref[0]


def _ada(c, c_ctx, ada_w, ada_b):
    depth = ada_w.shape[0]
    n = ada_w.shape[2]
    cb = jnp.stack([jnp.broadcast_to(c.reshape(D, 1), (D, 128)),
                    jnp.broadcast_to(c_ctx.reshape(D, 1), (D, 128))])
    return pl.pallas_call(
        _ada_kernel,
        grid=(depth, n // ADA_BN),
        in_specs=[pl.BlockSpec((2, D, 128), lambda l, j: (0, 0, 0)),
                  pl.BlockSpec((1, D, ADA_BN), lambda l, j: (l, 0, j)),
                  pl.BlockSpec((1, 1, ADA_BN), lambda l, j: (l, 0, j))],
        out_specs=pl.BlockSpec((1, 8, ADA_BN), lambda l, j: (l, 0, j)),
        out_shape=jax.ShapeDtypeStruct((depth, 8, n), F32),
        compiler_params=_cp(("arbitrary", "arbitrary")),
        name="ada",
    )(cb, ada_w, ada_b.reshape(depth, 1, n))


W_IN_COLS = Q_LORA + KV_LORA + 128 + FW


def _rope_chunk(x, cos, sin):
    lane = lax.broadcasted_iota(jnp.int32, x.shape, 1)
    sw = jnp.where((lane % 32) < 16, pltpu.roll(x, 112, 1), pltpu.roll(x, 16, 1))
    return x * cos + sw * sin


def _inproj_kernel(x_ref, mod_ref, g_ref, win_ref, qg_ref, wq_ref, kg_ref, wk_ref, wv_ref,
                   cos_ref, sin_ref, dft_ref, q_out, k_out, v_out, z_out):
    i = pl.program_id(0)
    is_ctx = i >= S // TM
    mod = jnp.where(is_ctx, mod_ref[0, 1:2, :], mod_ref[0, 0:1, :])
    h = _rms(x_ref[...]) * g_ref[...] * (1.0 + mod[:, D:2 * D]) + mod[:, 0:D]
    u = jnp.dot(h.astype(BF16), win_ref[...], preferred_element_type=F32)
    cos = cos_ref[...]
    sin = sin_ref[...]
    qc = (_rms(u[:, 0:Q_LORA]) * qg_ref[...]).astype(BF16)
    q = jnp.dot(qc, wq_ref[...], preferred_element_type=F32)
    for hd in range(HEADS):
        q_out[:, hd * HQ:hd * HQ + NOPE] = (q[:, hd * HQ:hd * HQ + NOPE] * QSCALE).astype(BF16)
        qr = _rope_chunk(q[:, hd * HQ + NOPE:(hd + 1) * HQ], cos, sin) * QSCALE
        q_out[:, hd * HQ + NOPE:(hd + 1) * HQ] = qr.astype(BF16)
    kc = (_rms(u[:, Q_LORA:Q_LORA + KV_LORA]) * kg_ref[...]).astype(BF16)
    kn = jnp.dot(kc, wk_ref[...], preferred_element_type=F32)
    v = jnp.dot(kc, wv_ref[...], preferred_element_type=F32)
    v_out[...] = jnp.transpose(v).astype(BF16)
    kr = _rope_chunk(u[:, 2 * Q_LORA:2 * Q_LORA + 128], cos, sin).astype(BF16)
    for hd in range(HEADS):
        k_out[:, hd * HQ:hd * HQ + NOPE] = kn[:, hd * NOPE:(hd + 1) * NOPE].astype(BF16)
        k_out[:, hd * HQ + NOPE:(hd + 1) * HQ] = kr
    f0 = 2 * Q_LORA + 128
    for g in range(FW // FG):
        fg = u[:, f0 + g * FG:f0 + (g + 1) * FG].astype(BF16)
        zz = jnp.dot(fg, dft_ref[...], preferred_element_type=F32)
        z_out[0, :, g * FG:(g + 1) * FG] = zz[:, 0:FG].astype(BF16)
        z_out[1, :, g * FG:(g + 1) * FG] = zz[:, FG:2 * FG].astype(BF16)


def _inproj(xc, mods, g1, w_in, qg, wq, kg, wk, wv, cos_t, sin_t, dft):
    nt = TT // TM
    row = lambda i: (i, 0)
    return pl.pallas_call(
        _inproj_kernel,
        grid=(nt,),
        in_specs=[pl.BlockSpec((TM, D), row),
                  pl.BlockSpec((1, 8, 2 * D), lambda i: (0, 0, 0)),
                  _const_spec((1, D)),
                  _const_spec((D, W_IN_COLS)),
                  _const_spec((1, Q_LORA)),
                  _const_spec((Q_LORA, HEADS * HQ)),
                  _const_spec((1, KV_LORA)),
                  _const_spec((KV_LORA, HEADS * NOPE)),
                  _const_spec((KV_LORA, HEADS * VH)),
                  pl.BlockSpec((TM, 128), row),
                  pl.BlockSpec((TM, 128), row),
                  _const_spec((FG, 2 * FG))],
        out_specs=[pl.BlockSpec((TM, HEADS * HQ), row),
                   pl.BlockSpec((TM, HEADS * HQ), row),
                   pl.BlockSpec((HEADS * VH, TM), lambda i: (0, i)),
                   pl.BlockSpec((2, TM, FW), lambda i: (0, i, 0))],
        out_shape=[jax.ShapeDtypeStruct((TT, HEADS * HQ), BF16),
                   jax.ShapeDtypeStruct((TT, HEADS * HQ), BF16),
                   jax.ShapeDtypeStruct((HEADS * VH, TT), BF16),
                   jax.ShapeDtypeStruct((2, TT, FW), BF16)],
        compiler_params=_cp(("arbitrary",)),
        name="inproj",
    )(xc, mods, g1, w_in, qg, wq, kg, wk, wv, cos_t, sin_t, dft)


ATT_CK = 2816


def _attn_kernel(q_ref, k_ref, vt_ref, o_ref):
    q = q_ref[...]
    m = jnp.full((1, TQ), -jnp.inf, F32)
    l = jnp.zeros((1, TQ), F32)
    acc = jnp.zeros((VH, TQ), F32)
    nck = TT // ATT_CK

    def scores(c):
        return lax.dot_general(k_ref[c * ATT_CK:(c + 1) * ATT_CK, :], q, (((1,), (1,)), ((), ())),
                               preferred_element_type=F32)

    s_next = scores(0)
    for c in range(nck):
        ks = slice(c * ATT_CK, (c + 1) * ATT_CK)
        s = s_next
        if c + 1 < nck:
            s_next = scores(c + 1)
        m_new = jnp.maximum(m, jnp.max(s, axis=0, keepdims=True))
        alpha = jnp.exp2(m - m_new)
        p = jnp.exp2(s - m_new)
        l = l * alpha + jnp.sum(p, axis=0, keepdims=True)
        acc = acc * alpha + jnp.dot(vt_ref[:, ks], p.astype(BF16), preferred_element_type=F32)
        m = m_new
    o_ref[...] = jnp.transpose(acc / l).astype(BF16)


def _attention(qp, kp, vt):
    return pl.pallas_call(
        _attn_kernel,
        grid=(HEADS, S // TQ),
        in_specs=[pl.BlockSpec((TQ, HQ), lambda h, i: (i, h)),
                  pl.BlockSpec((TT, HQ), lambda h, i: (0, h)),
                  pl.BlockSpec((VH, TT), lambda h, i: (h, 0))],
        out_specs=pl.BlockSpec((TQ, VH), lambda h, i: (i, h)),
        out_shape=jax.ShapeDtypeStruct((S, HEADS * VH), BF16),
        compiler_params=_cp(("arbitrary", "arbitrary")),
        name="attention",
    )(qp, kp, vt)


FFT1_BN = 2048


def _fft1_kernel(m_ref, z_ref, y_ref):
    y_ref[...] = jnp.dot(m_ref[...], z_ref[...], preferred_element_type=F32).astype(BF16)


def _fft2_kernel(g_ref, y_ref, o_ref):
    y = y_ref[...].reshape(2 * FFT_N2, FW)
    o_ref[0] = jnp.dot(g_ref[0], y, preferred_element_type=F32).astype(BF16)


def _fft_tables():
    k1 = np.arange(FFT_N1, dtype=np.float64)
    a = 2.0 * np.pi * np.outer(k1, k1) / FFT_N1
    c1, s1 = np.cos(a), np.sin(a)
    m1 = np.block([[c1, s1], [-s1, c1]])
    t2 = np.arange(FFT_N2, dtype=np.float64)
    k2 = np.arange(FFT_N2, dtype=np.float64)
    th = 2.0 * np.pi * (k1[:, None, None] * t2[None, None, :] / (FFT_N1 * FFT_N2)
                        + k2[None, :, None] * t2[None, None, :] / FFT_N2)
    g = np.concatenate([np.cos(th), np.sin(th)], axis=2) / np.sqrt(float(S * FG))
    cc = np.arange(FG, dtype=np.float64)
    ac = 2.0 * np.pi * np.outer(cc, cc) / FG
    dft = np.concatenate([np.cos(ac), -np.sin(ac)], axis=1)
    return (jnp.asarray(m1, F32).astype(BF16), jnp.asarray(g, F32).astype(BF16),
            jnp.asarray(dft, F32).astype(BF16))


def _fourier_tokens(z, m1, gtab):
    ncol = FFT_N2 * FW
    z2 = z[:, :S].reshape(2 * FFT_N1, ncol)
    y = pl.pallas_call(
        _fft1_kernel,
        grid=(ncol // FFT1_BN,),
        in_specs=[_const_spec((2 * FFT_N1, 2 * FFT_N1)),
                  pl.BlockSpec((2 * FFT_N1, FFT1_BN), lambda j: (0, j))],
        out_specs=pl.BlockSpec((2 * FFT_N1, FFT1_BN), lambda j: (0, j)),
        out_shape=jax.ShapeDtypeStruct((2 * FFT_N1, ncol), BF16),
        compiler_params=_cp(("arbitrary",)),
        name="fft1",
    )(m1, z2)
    y4 = y.reshape(2, FFT_N1, FFT_N2, FW)
    o = pl.pallas_call(
        _fft2_kernel,
        grid=(FFT_N1,),
        in_specs=[pl.BlockSpec((1, FFT_N2, 2 * FFT_N2), lambda k: (k, 0, 0)),
                  pl.BlockSpec((2, None, FFT_N2, FW), lambda k: (0, k, 0, 0))],
        out_specs=pl.BlockSpec((1, FFT_N2, FW), lambda k: (k, 0, 0)),
        out_shape=jax.ShapeDtypeStruct((FFT_N1, FFT_N2, FW), BF16),
        compiler_params=_cp(("arbitrary",)),
        name="fft2",
    )(gtab, y4)
    return jnp.transpose(o, (1, 0, 2)).reshape(S, FW)


def _route(h2, rw_ref, rb_ref, ids_ref, wts_ref):
    hi = h2.astype(BF16)
    lo = (h2 - hi.astype(F32)).astype(BF16)
    r = jnp.dot(jnp.concatenate([hi, lo], axis=0), rw_ref[...], preferred_element_type=F32)
    logits = (r[0:TM, 0:128] + r[0:TM, 128:256]) + (r[TM:2 * TM, 0:128] + r[TM:2 * TM, 128:256])
    lt = jnp.transpose(logits)[0:N_E, :]
    sc = jax.nn.sigmoid(lt)
    bi = sc + rb_ref[...]
    sj = [sc[j * N_G:(j + 1) * N_G] for j in range(EPG)]
    bj = [bi[j * N_G:(j + 1) * N_G] for j in range(EPG)]
    m01, n01 = jnp.maximum(bj[0], bj[1]), jnp.minimum(bj[0], bj[1])
    m23, n23 = jnp.maximum(bj[2], bj[3]), jnp.minimum(bj[2], bj[3])
    gs = jnp.maximum(m01, m23) + jnp.maximum(jnp.minimum(m01, m23), jnp.maximum(n01, n23))
    gmax = jnp.max(gs, axis=0, keepdims=True)
    gi = lax.broadcasted_iota(jnp.int32, gs.shape, 0)
    best = jnp.min(jnp.where(gs == gmax, gi, N_G), axis=0, keepdims=True)
    sel = gi == best
    b = [jnp.sum(jnp.where(sel, x, 0.0), axis=0, keepdims=True) for x in bj]
    s = [jnp.sum(jnp.where(sel, x, 0.0), axis=0, keepdims=True) for x in sj]
    b1 = jnp.maximum(jnp.maximum(b[0], b[1]), jnp.maximum(b[2], b[3]))
    j1 = jnp.full(b1.shape, EPG, jnp.int32)
    for j in reversed(range(EPG)):
        j1 = jnp.where(b[j] == b1, j, j1)
    c = [jnp.where(j1 == j, -jnp.inf, b[j]) for j in range(EPG)]
    b2 = jnp.maximum(jnp.maximum(c[0], c[1]), jnp.maximum(c[2], c[3]))
    j2 = jnp.full(b1.shape, EPG, jnp.int32)
    for j in reversed(range(EPG)):
        j2 = jnp.where(c[j] == b2, j, j2)
    s1 = sum(jnp.where(j1 == j, s[j], 0.0) for j in range(EPG))
    s2 = sum(jnp.where(j2 == j, s[j], 0.0) for j in range(EPG))
    tot = s1 + s2
    zi = jnp.zeros((6,) + b1.shape[1:], jnp.int32)
    zf = jnp.zeros((6,) + b1.shape[1:], F32)
    ids_ref[...] = jnp.concatenate([best * EPG + j1, best * EPG + j2, zi], axis=0)
    wts_ref[...] = jnp.concatenate([s1 / tot, s2 / tot, zf], axis=0)


def _tail(y, x_ref, mod_ref, g2_ref, rw_ref, rb_ref, x1_ref, h2_ref, ids_ref, wts_ref):
    mod = mod_ref[0, 0:1, :]
    x1 = x_ref[...] + mod[:, 0:D] * y
    x1_ref[...] = x1
    h2 = _rms(x1) * g2_ref[...] * (1.0 + mod[:, 2 * D:3 * D]) + mod[:, D:2 * D]
    _store_rows(h2_ref, h2, TM)
    _route(h2, rw_ref, rb_ref, ids_ref, wts_ref)


def _mix_tail_kernel(a_ref, f_ref, wa_ref, wf_ref, x_ref, mod_ref, g2_ref, rw_ref, rb_ref,
                     x1_ref, h2_ref, ids_ref, wts_ref):
    y = (jnp.dot(a_ref[...], wa_ref[...], preferred_element_type=F32)
         + jnp.dot(f_ref[...], wf_ref[...], preferred_element_type=F32))
    _tail(y, x_ref, mod_ref, g2_ref, rw_ref, rb_ref, x1_ref, h2_ref, ids_ref, wts_ref)


def _tail_specs():
    row = lambda i: (i, 0)
    in_specs = [pl.BlockSpec((TM, D), row),
                pl.BlockSpec((1, 8, 3 * D), lambda i: (0, 0, 0)),
                _const_spec((1, D)),
                _const_spec((D, 256)),
                _const_spec((N_E, 1))]
    out_specs = [pl.BlockSpec((TM, D), row),
                 pl.BlockSpec((TM * SLABS, 128), row),
                 pl.BlockSpec((8, TM), lambda i: (0, i)),
                 pl.BlockSpec((8, TM), lambda i: (0, i))]
    out_shape = [jax.ShapeDtypeStruct((S, D), F32),
                 jax.ShapeDtypeStruct((S * SLABS, 128), F32),
                 jax.ShapeDtypeStruct((8, S), jnp.int32),
                 jax.ShapeDtypeStruct((8, S), F32)]
    return in_specs, out_specs, out_shape


def _mix_tail(attn, four, wa, wf, x, mod, g2, rw, rb):
    row = lambda i: (i, 0)
    tin, tout, tshape = _tail_specs()
    return pl.pallas_call(
        _mix_tail_kernel,
        grid=(S // TM,),
        in_specs=[pl.BlockSpec((TM, HEADS * VH), row),
                  pl.BlockSpec((TM, FW), row),
                  _const_spec((HEADS * VH, D)),
                  _const_spec((FW, D))] + tin,
        out_specs=tout,
        out_shape=tshape,
        compiler_params=_cp(("arbitrary",)),
        name="mix_tail",
    )(attn, four, wa, wf, x, mod, g2, rw, rb)


def _conv_in_kernel(h_ref, w_ref, b_ref, cz_ref):
    u = jnp.dot(h_ref[...], w_ref[...], preferred_element_type=F32)
    b_ref[...] = u[:, 0:D].astype(BF16)
    cz_ref[...] = (u[:, D:2 * D] * u[:, 2 * D:3 * D]).astype(BF16)


def _conv_in(h, w):
    row = lambda i: (i, 0)
    return pl.pallas_call(
        _conv_in_kernel,
        grid=(S // TM,),
        in_specs=[pl.BlockSpec((TM, D), row), _const_spec((D, 3 * D))],
        out_specs=[pl.BlockSpec((TM, D), row), pl.BlockSpec((TM, D), row)],
        out_shape=[jax.ShapeDtypeStruct((S, D), BF16), jax.ShapeDtypeStruct((S, D), BF16)],
        compiler_params=_cp(("arbitrary",)),
        name="conv_in",
    )(h, w)


def _conv_tail_kernel(b_ref, cz_ref, prev_ref, next_ref, cw_ref, wo_ref, x_ref, mod_ref, g2_ref,
                      rw_ref, rb_ref, x1_ref, h2_ref, ids_ref, wts_ref):
    i = pl.program_id(0)
    cz = cz_ref[...].astype(F32)
    rid = lax.broadcasted_iota(jnp.int32, cz.shape, 0)
    prev_row = jnp.where(i > 0, prev_ref[15:16, :].astype(F32), 0.0)
    next_row = jnp.where(i < S // TM - 1, next_ref[0:1, :].astype(F32), 0.0)
    dn = jnp.where(rid == 0, prev_row, pltpu.roll(cz, 1, 0))
    up = jnp.where(rid == TM - 1, next_row, pltpu.roll(cz, TM - 1, 0))
    y = dn * cw_ref[0:1, :] + cz * cw_ref[1:2, :] + up * cw_ref[2:3, :]
    yb = (b_ref[...].astype(F32) * y).astype(BF16)
    y = jnp.dot(yb, wo_ref[...], preferred_element_type=F32)
    _tail(y, x_ref, mod_ref, g2_ref, rw_ref, rb_ref, x1_ref, h2_ref, ids_ref, wts_ref)


def _conv_tail(bg, cz, cw, wo, x, mod, g2, rw, rb):
    row = lambda i: (i, 0)
    tin, tout, tshape = _tail_specs()
    nb = TM // 16
    return pl.pallas_call(
        _conv_tail_kernel,
        grid=(S // TM,),
        in_specs=[pl.BlockSpec((TM, D), row),
                  pl.BlockSpec((TM, D), row),
                  pl.BlockSpec((16, D), lambda i: (jnp.maximum(i * nb - 1, 0), 0)),
                  pl.BlockSpec((16, D), lambda i: (jnp.minimum((i + 1) * nb, S // 16 - 1), 0)),
                  _const_spec((8, D)),
                  _const_spec((D, D))] + tin,
        out_specs=tout,
        out_shape=tshape,
        compiler_params=_cp(("arbitrary",)),
        name="conv_tail",
    )(bg, cz, cz, cz, cw, wo, x, mod, g2, rw, rb)


def _row_slab(ref, r):
    return ref.at[pl.ds(pl.multiple_of(r * SLABS, SLABS), SLABS)]


def _scatter_kernel(pos_ref, h_ref, init_ref, xs_ref, sem):
    del init_ref

    def dma(j):
        return pltpu.make_async_copy(_row_slab(h_ref, j // 2), _row_slab(xs_ref, pos_ref[0, j]), sem)

    def issue(j, carry):
        dma(j).start()
        return carry

    def drain(j, carry):
        dma(j).wait()
        return carry

    lax.fori_loop(0, 2 * TM, issue, 0, unroll=8)
    lax.fori_loop(0, 2 * TM, drain, 0, unroll=8)


def _scatter_rows(h2, pos):
    nt = S // TM
    return pl.pallas_call(
        _scatter_kernel,
        grid=(nt,),
        in_specs=[pl.BlockSpec((None, 1, 2 * TM), lambda i: (i, 0, 0), memory_space=pltpu.SMEM),
                  pl.BlockSpec((TM * SLABS, 128), lambda i: (i, 0)),
                  pl.BlockSpec(memory_space=pl.ANY)],
        out_specs=pl.BlockSpec(memory_space=pl.ANY),
        out_shape=jax.ShapeDtypeStruct((P_ROWS * SLABS, 128), F32),
        scratch_shapes=[pltpu.SemaphoreType.DMA(())],
        input_output_aliases={2: 0},
        compiler_params=pltpu.CompilerParams(dimension_semantics=("arbitrary",),
                                             has_side_effects=True),
        name="scatter_rows",
    )(pos.reshape(nt, 1, 2 * TM), h2, jnp.zeros((P_ROWS * SLABS, 128), F32))


def _moe_kernel(te_ref, first_ref, nused_ref, x_ref, wg_ref, wu_ref, wd_ref, o_ref,
                wg_s, wu_s, wd_s):
    i = pl.program_id(0)

    @pl.when(first_ref[i] == 1)
    def _():
        wg_s[...] = wg_ref[...].astype(BF16)
        wu_s[...] = wu_ref[...].astype(BF16)
        wd_s[...] = wd_ref[...].astype(BF16)

    @pl.when(i < nused_ref[0])
    def _():
        x = _load_rows(x_ref, MOE_TM).astype(BF16)
        g = jnp.dot(x, wg_s[...], preferred_element_type=F32)
        u = jnp.dot(x, wu_s[...], preferred_element_type=F32)
        a = (_silu(g) * u).astype(BF16)
        _store_rows(o_ref, jnp.dot(a, wd_s[...], preferred_element_type=F32), MOE_TM)

    @pl.when(i >= nused_ref[0])
    def _():
        o_ref[...] = jnp.zeros(o_ref.shape, F32)


def _moe(te, first, nused, xs, w_gate, w_up, w_down, layer):
    def xmap(i, te, first, nused):
        return (jnp.minimum(i, nused[0] - 1), 0)

    def omap(i, te, first, nused):
        return (i, 0)

    wmap = lambda i, te, first, nused: (layer, te[i], 0, 0)
    grid_spec = pltpu.PrefetchScalarGridSpec(
        num_scalar_prefetch=3,
        grid=(MOE_NT,),
        in_specs=[pl.BlockSpec((MOE_TM * SLABS, 128), xmap),
                  pl.BlockSpec((None, None, D, D_E), wmap),
                  pl.BlockSpec((None, None, D, D_E), wmap),
                  pl.BlockSpec((None, None, D_E, D), wmap)],
        out_specs=pl.BlockSpec((MOE_TM * SLABS, 128), omap),
        scratch_shapes=[pltpu.VMEM((D, D_E), BF16), pltpu.VMEM((D, D_E), BF16),
                        pltpu.VMEM((D_E, D), BF16)],
    )
    return pl.pallas_call(
        _moe_kernel,
        grid_spec=grid_spec,
        out_shape=jax.ShapeDtypeStruct((P_ROWS * SLABS, 128), F32),
        compiler_params=_cp(("arbitrary",)),
        name="moe",
    )(te, first, nused, xs, w_gate, w_up, w_down)


def _route_plan(ids):
    ef = ids[0:2].T.reshape(-1)
    oh = (ef[:, None] == jnp.arange(N_E, dtype=jnp.int32)[None, :]).astype(jnp.int32)
    csum = jnp.cumsum(oh, axis=0)
    rank = jnp.sum(csum * oh, axis=1) - 1
    cnt = csum[-1]
    pc = ((cnt + MOE_TM - 1) // MOE_TM) * MOE_TM
    end = jnp.cumsum(pc)
    off = end - pc
    pos = jnp.sum(oh * off[None, :], axis=1) + rank
    nused = (end[-1] // MOE_TM).astype(jnp.int32)
    ti = jnp.arange(MOE_NT, dtype=jnp.int32)
    te = jnp.sum((ti[:, None] >= (end // MOE_TM)[None, :]).astype(jnp.int32), axis=1)
    te_last = jnp.sum((nused - 1 >= end // MOE_TM).astype(jnp.int32))
    te = jnp.minimum(te, te_last).astype(jnp.int32)
    first = jnp.concatenate([jnp.ones((1,), jnp.int32), (te[1:] != te[:-1]).astype(jnp.int32)])
    return pos.astype(jnp.int32), te, first, nused.reshape(1)


def _moe_block(h2, ids, w_gate, w_up, w_down, layer):
    pos, te, first, nused = _route_plan(ids)
    xs = _scatter_rows(h2, pos)
    ys = _moe(te, first, nused, xs, w_gate, w_up, w_down, layer)
    return ys, pos.reshape(S // TM, 1, 2 * TM)


def _combine(pos_ref, npos_ref, ys_ref, x1_ref, w_ref, mod_ref, buf, sems):
    i = pl.program_id(0)
    slot = i % 2

    def dma(p_ref, st, r, k):
        return pltpu.make_async_copy(_row_slab(ys_ref, p_ref[0, 2 * r + k]),
                                     _row_slab(buf.at[st, k], r), sems.at[st])

    def issue(p_ref, st):
        def body(r, carry):
            dma(p_ref, st, r, 0).start()
            dma(p_ref, st, r, 1).start()
            return carry
        lax.fori_loop(0, TM, body, 0, unroll=8)

    @pl.when(i == 0)
    def _():
        issue(pos_ref, 0)

    @pl.when(i + 1 < pl.num_programs(0))
    def _():
        issue(npos_ref, 1 - slot)

    def drain(r, carry):
        dma(pos_ref, slot, r, 0).wait()
        dma(pos_ref, slot, r, 1).wait()
        return carry

    lax.fori_loop(0, TM, drain, 0, unroll=8)
    w = w_ref[...]
    moe = w[:, 0:1] * _load_rows(buf.at[slot, 0], TM) + w[:, 1:2] * _load_rows(buf.at[slot, 1], TM)
    return x1_ref[...] + mod_ref[0, 0:1, :] * moe


def _fuse_next_kernel(pos_ref, npos_ref, ys_ref, x1_ref, w_ref, mod_ref, nmod_ref, g_ref, x2_ref, h_ref,
                      buf, sems):
    x2 = _combine(pos_ref, npos_ref, ys_ref, x1_ref, w_ref, mod_ref, buf, sems)
    x2_ref[...] = x2
    nm = nmod_ref[0, 0:1, :]
    h_ref[...] = (_rms(x2) * g_ref[...] * (1.0 + nm[:, D:2 * D]) + nm[:, 0:D]).astype(BF16)


def _fuse_final_kernel(pos_ref, npos_ref, ys_ref, x1_ref, w_ref, mod_ref, g_ref, o_ref, buf, sems):
    x2 = _combine(pos_ref, npos_ref, ys_ref, x1_ref, w_ref, mod_ref, buf, sems)
    o_ref[...] = _rms(x2) * g_ref[...]


def _fuse_specs():
    row = lambda i: (i, 0)
    last = S // TM - 1
    return [pl.BlockSpec((None, 1, 2 * TM), lambda i: (i, 0, 0), memory_space=pltpu.SMEM),
            pl.BlockSpec((None, 1, 2 * TM), lambda i: (jnp.minimum(i + 1, last), 0, 0),
                         memory_space=pltpu.SMEM),
            pl.BlockSpec(memory_space=pl.ANY),
            pl.BlockSpec((TM, D), row),
            pl.BlockSpec((TM, 8), row),
            pl.BlockSpec((1, 8, D), lambda i: (0, 0, 0))]


def _fuse_scratch():
    return [pltpu.VMEM((2, 2, TM * SLABS, 128), F32), pltpu.SemaphoreType.DMA((2,))]


def _fuse_next(pos, ys, x1, wt, mod_g2, nmod, g):
    row = lambda i: (i, 0)
    return pl.pallas_call(
        _fuse_next_kernel,
        grid=(S // TM,),
        in_specs=_fuse_specs() + [pl.BlockSpec((1, 8, 2 * D), lambda i: (0, 0, 0)),
                                  _const_spec((1, D))],
        out_specs=[pl.BlockSpec((TM, D), row), pl.BlockSpec((TM, D), row)],
        out_shape=[jax.ShapeDtypeStruct((S, D), F32), jax.ShapeDtypeStruct((S, D), BF16)],
        scratch_shapes=_fuse_scratch(),
        compiler_params=_cp(("arbitrary",)),
        name="fuse_next",
    )(pos, pos, ys, x1, wt, mod_g2, nmod, g)


def _fuse_final(pos, ys, x1, wt, mod_g2, g):
    row = lambda i: (i, 0)
    return pl.pallas_call(
        _fuse_final_kernel,
        grid=(S // TM,),
        in_specs=_fuse_specs() + [_const_spec((1, D))],
        out_specs=pl.BlockSpec((TM, D), row),
        out_shape=jax.ShapeDtypeStruct((S, D), F32),
        scratch_shapes=_fuse_scratch(),
        compiler_params=_cp(("arbitrary",)),
        name="fuse_final",
    )(pos, pos, ys, x1, wt, mod_g2, g)


def _rope_tables():
    half = ROPE // 2
    row = jnp.repeat(jnp.arange(S // GRID_W), GRID_W).astype(F32)
    col = jnp.tile(jnp.arange(GRID_W), S // GRID_W).astype(F32)
    inv = THETA ** (-jnp.arange(0, half, 2, dtype=F32) / half)
    ang_r = row[:, None] * inv
    ang_c = col[:, None] * inv
    cr, sr, cc, sc = jnp.cos(ang_r), jnp.sin(ang_r), jnp.cos(ang_c), jnp.sin(ang_c)
    one = jnp.ones((S, 64), F32)
    zero = jnp.zeros((S, 64), F32)
    cos_t = jnp.concatenate([cr, cr, cc, cc, one], axis=1)
    sin_t = jnp.concatenate([-sr, sr, -sc, sc, zero], axis=1)
    cos_t = jnp.concatenate([cos_t, jnp.ones((CTX, 128), F32)], axis=0)
    sin_t = jnp.concatenate([sin_t, jnp.zeros((CTX, 128), F32)], axis=0)
    return cos_t, sin_t


def _prep_layer0(attn_in_w, q_up_w, kv_up_w):
    c0, c1, c2 = Q_LORA, Q_LORA + KV_LORA, Q_LORA + KV_LORA + ROPE
    w_in = jnp.concatenate([attn_in_w[:, :c2], jnp.zeros((D, 128 - ROPE), F32), attn_in_w[:, c2:]],
                           axis=1).astype(BF16)
    qw = q_up_w.reshape(Q_LORA, HEADS, NOPE + ROPE)
    wq = jnp.concatenate([qw, jnp.zeros((Q_LORA, HEADS, HQ - NOPE - ROPE), F32)],
                         axis=2).reshape(Q_LORA, HEADS * HQ).astype(BF16)
    kvw = kv_up_w.reshape(KV_LORA, HEADS, NOPE + VH)
    wk = kvw[:, :, :NOPE].reshape(KV_LORA, HEADS * NOPE).astype(BF16)
    wv = kvw[:, :, NOPE:].reshape(KV_LORA, HEADS * VH).astype(BF16)
    del c0, c1
    return w_in, wq, wk, wv


def _prep_router(router_w, router_b):
    perm = np.array([EPG * (s % N_G) + s // N_G for s in range(N_E)])
    rw = jnp.concatenate([router_w[:, perm], jnp.zeros((D, 128 - N_E), F32)], axis=1)
    rw_hi = rw.astype(BF16)
    rw_lo = (rw - rw_hi.astype(F32)).astype(BF16)
    rb = router_b[perm].reshape(N_E, 1)
    return jnp.concatenate([rw_hi, rw_lo], axis=1), rb


def kernel(x, c, ctx, c_ctx, ada_w, ada_b, norm1_g, norm2_g, attn_in_w, q_norm_g, q_up_w, kv_norm_g,
           kv_up_w, conv_in_w, conv_w, mix_out_w, router_w, router_b, moe_w_gate, moe_w_up,
           moe_w_down, final_norm_g):
    x = x.reshape(S, D)
    mods = _ada(c.reshape(D), c_ctx, ada_w, ada_b)
    rw, rb = _prep_router(router_w, router_b)
    m1, gtab, dft = _fft_tables()
    cos_t, sin_t = _rope_tables()

    w_in, wq, wk, wv = _prep_layer0(attn_in_w[0], q_up_w[0], kv_up_w[0])
    xc = jnp.concatenate([x, ctx.reshape(CTX, D)], axis=0)
    qp, kp, vt, z = _inproj(xc, mods[0:1, :, 0:2 * D], norm1_g[0:1], w_in, q_norm_g[0:1], wq,
                           kv_norm_g[0:1], wk, wv, cos_t, sin_t, dft)
    attn = _attention(qp, kp, vt)
    four = _fourier_tokens(z, m1, gtab)
    wo = mix_out_w[0].astype(BF16)
    x1, h2, ids, wts = _mix_tail(attn, four, wo[:HEADS * VH], wo[HEADS * VH:], x,
                                 mods[0:1, :, 2 * D:5 * D], norm2_g[0:1], rw, rb)
    ys, pos = _moe_block(h2, ids, moe_w_gate, moe_w_up, moe_w_down, 0)
    x2, h = _fuse_next(pos, ys, x1, wts.T, mods[0:1, :, 5 * D:6 * D], mods[1:2, :, 0:2 * D],
                       norm1_g[1:2])

    bg, cz = _conv_in(h, conv_in_w[0].astype(BF16))
    cw = jnp.concatenate([conv_w[0], jnp.zeros((5, D), F32)], axis=0)
    x1, h2, ids, wts = _conv_tail(bg, cz, cw, mix_out_w[1].astype(BF16), x2,
                                  mods[1:2, :, 2 * D:5 * D], norm2_g[1:2], rw, rb)
    ys, pos = _moe_block(h2, ids, moe_w_gate, moe_w_up, moe_w_down, 1)
    out = _fuse_final(pos, ys, x1, wts.T, mods[1:2, :, 5 * D:6 * D], final_norm_g.reshape(1, D))
    return out.reshape(1, S, D)
```

```python
import functools

import numpy as np
import jax
import jax.numpy as jnp
from jax import lax
from jax.experimental import pallas as pl
from jax.experimental.pallas import tpu as pltpu

F32 = jnp.float32
BF16 = jnp.bfloat16

D = 2048
S = 8192
CTX = 256
TT = S + CTX
GRID_W = 64
HEADS = 8
Q_LORA = 512
KV_LORA = 512
NOPE = 128
ROPE = 64
VH = 128
HQ = 256
THETA = 10000.0
QSCALE = (NOPE + ROPE) ** -0.5 * float(np.log2(np.e))
FW = 1024
FG = 128
N_E = 32
N_G = 8
EPG = 4
D_E = 512
EPS = 1e-6
NMOD = 6

TM = 256
TQ = 256
MOE_TM = 128
P_ROWS = 2 * S + N_E * MOE_TM
MOE_NT = P_ROWS // MOE_TM
FFT_N1 = 128
FFT_N2 = 64

VMEM_LIMIT = 56 * 1024 * 1024


def _cp(sem, limit=VMEM_LIMIT):
    return pltpu.CompilerParams(dimension_semantics=sem, vmem_limit_bytes=limit)


def _const_spec(shape):
    n = len(shape)
    return pl.BlockSpec(shape, lambda *a: (0,) * n, pipeline_mode=pl.Buffered(1))


SLABS = D // 128


def _load_rows(ref, n):
    return jnp.concatenate([ref[pl.ds(j, n, stride=SLABS), :] for j in range(SLABS)], axis=1)


def _store_rows(ref, val, n):
    for j in range(SLABS):
        ref[pl.ds(j, n, stride=SLABS), :] = val[:, j * 128:(j + 1) * 128]


def _rms(x):
    return x * lax.rsqrt(jnp.mean(x * x, axis=-1, keepdims=True) + EPS)


def _silu(x):
    return x * jax.nn.sigmoid(x)


ADA_BN = 1024


def _ada_kernel(cb_ref, w_ref, b_ref, o_ref):
    a0 = _silu(cb_ref[0])
    a1 = _silu(cb_ref[1])
    rows = []
    for j in range(ADA_BN // 128):
        w = w_ref[0, :, j * 128:(j + 1) * 128]
        r0 = jnp.sum(w * a0, axis=0, keepdims=True)
        r1 = jnp.sum(w * a1, axis=0, keepdims=True)
        rows.append(jnp.concatenate([r0, r1, jnp.zeros((6, 128), F32)], axis=0))
    o_ref[0] = jnp.concatenate(rows, axis=1) + b_ref[0]


def _ada(c, c_ctx, ada_w, ada_b):
    depth = ada_w.shape[0]
    n = ada_w.shape[2]
    cb = jnp.stack([jnp.broadcast_to(c.reshape(D, 1), (D, 128)),
                    jnp.broadcast_to(c_ctx.reshape(D, 1), (D, 128))])
    return pl.pallas_call(
        _ada_kernel,
        grid=(depth, n // ADA_BN),
        in_specs=[pl.BlockSpec((2, D, 128), lambda l, j: (0, 0, 0)),
                  pl.BlockSpec((1, D, ADA_BN), lambda l, j: (l, 0, j)),
                  pl.BlockSpec((1, 1, ADA_BN), lambda l, j: (l, 0, j))],
        out_specs=pl.BlockSpec((1, 8, ADA_BN), lambda l, j: (l, 0, j)),
        out_shape=jax.ShapeDtypeStruct((depth, 8, n), F32),
        compiler_params=_cp(("arbitrary", "arbitrary")),
        name="ada",
    )(cb, ada_w, ada_b.reshape(depth, 1, n))


W_IN_COLS = Q_LORA + KV_LORA + 128 + FW


def _rope_chunk(x, cos, sin):
    lane = lax.broadcasted_iota(jnp.int32, x.shape, 1)
    sw = jnp.where((lane % 32) < 16, pltpu.roll(x, 112, 1), pltpu.roll(x, 16, 1))
    return x * cos + sw * sin


def _inproj_kernel(x_ref, mod_ref, g_ref, win_ref, qg_ref, wq_ref, kg_ref, wk_ref, wv_ref,
                   cos_ref, sin_ref, dft_ref, q_out, k_out, v_out, z_out):
    i = pl.program_id(0)
    is_ctx = i >= S // TM
    mod = jnp.where(is_ctx, mod_ref[0, 1:2, :], mod_ref[0, 0:1, :])
    h = _rms(x_ref[...]) * g_ref[...] * (1.0 + mod[:, D:2 * D]) + mod[:, 0:D]
    u = jnp.dot(h.astype(BF16), win_ref[...], preferred_element_type=F32)
    cos = cos_ref[...]
    sin = sin_ref[...]
    qc = (_rms(u[:, 0:Q_LORA]) * qg_ref[...]).astype(BF16)
    q = jnp.dot(qc, wq_ref[...], preferred_element_type=F32)
    for hd in range(HEADS):
        q_out[:, hd * HQ:hd * HQ + NOPE] = (q[:, hd * HQ:hd * HQ + NOPE] * QSCALE).astype(BF16)
        qr = _rope_chunk(q[:, hd * HQ + NOPE:(hd + 1) * HQ], cos, sin) * QSCALE
        q_out[:, hd * HQ + NOPE:(hd + 1) * HQ] = qr.astype(BF16)
    kc = (_rms(u[:, Q_LORA:Q_LORA + KV_LORA]) * kg_ref[...]).astype(BF16)
    kn = jnp.dot(kc, wk_ref[...], preferred_element_type=F32)
    v = jnp.dot(kc, wv_ref[...], preferred_element_type=F32)
    v_out[...] = jnp.transpose(v).astype(BF16)
    kr = _rope_chunk(u[:, 2 * Q_LORA:2 * Q_LORA + 128], cos, sin).astype(BF16)
    for hd in range(HEADS):
        k_out[:, hd * HQ:hd * HQ + NOPE] = kn[:, hd * NOPE:(hd + 1) * NOPE].astype(BF16)
        k_out[:, hd * HQ + NOPE:(hd + 1) * HQ] = kr
    f0 = 2 * Q_LORA + 128
    for g in range(FW // FG):
        fg = u[:, f0 + g * FG:f0 + (g + 1) * FG].astype(BF16)
        zz = jnp.dot(fg, dft_ref[...], preferred_element_type=F32)
        z_out[0, :, g * FG:(g + 1) * FG] = zz[:, 0:FG].astype(BF16)
        z_out[1, :, g * FG:(g + 1) * FG] = zz[:, FG:2 * FG].astype(BF16)


def _inproj(xc, mods, g1, w_in, qg, wq, kg, wk, wv, cos_t, sin_t, dft):
    nt = TT // TM
    row = lambda i: (i, 0)
    return pl.pallas_call(
        _inproj_kernel,
        grid=(nt,),
        in_specs=[pl.BlockSpec((TM, D), row),
                  pl.BlockSpec((1, 8, 2 * D), lambda i: (0, 0, 0)),
                  _const_spec((1, D)),
                  _const_spec((D, W_IN_COLS)),
                  _const_spec((1, Q_LORA)),
                  _const_spec((Q_LORA, HEADS * HQ)),
                  _const_spec((1, KV_LORA)),
                  _const_spec((KV_LORA, HEADS * NOPE)),
                  _const_spec((KV_LORA, HEADS * VH)),
                  pl.BlockSpec((TM, 128), row),
                  pl.BlockSpec((TM, 128), row),
                  _const_spec((FG, 2 * FG))],
        out_specs=[pl.BlockSpec((TM, HEADS * HQ), row),
                   pl.BlockSpec((TM, HEADS * HQ), row),
                   pl.BlockSpec((HEADS * VH, TM), lambda i: (0, i)),
                   pl.BlockSpec((2, TM, FW), lambda i: (0, i, 0))],
        out_shape=[jax.ShapeDtypeStruct((TT, HEADS * HQ), BF16),
                   jax.ShapeDtypeStruct((TT, HEADS * HQ), BF16),
                   jax.ShapeDtypeStruct((HEADS * VH, TT), BF16),
                   jax.ShapeDtypeStruct((2, TT, FW), BF16)],
        compiler_params=_cp(("arbitrary",)),
        name="inproj",
    )(xc, mods, g1, w_in, qg, wq, kg, wk, wv, cos_t, sin_t, dft)


ATT_CK = 2816


def _attn_kernel(q_ref, k_ref, vt_ref, o_ref):
    q = q_ref[...]
    m = jnp.full((1, TQ), -jnp.inf, F32)
    l = jnp.zeros((1, TQ), F32)
    acc = jnp.zeros((VH, TQ), F32)
    nck = TT // ATT_CK

    def scores(c):
        return lax.dot_general(k_ref[c * ATT_CK:(c + 1) * ATT_CK, :], q, (((1,), (1,)), ((), ())),
                               preferred_element_type=F32)

    s_next = scores(0)
    for c in range(nck):
        ks = slice(c * ATT_CK, (c + 1) * ATT_CK)
        s = s_next
        if c + 1 < nck:
            s_next = scores(c + 1)
        m_new = jnp.maximum(m, jnp.max(s, axis=0, keepdims=True))
        alpha = jnp.exp2(m - m_new)
        p = jnp.exp2(s - m_new)
        l = l * alpha + jnp.sum(p, axis=0, keepdims=True)
        acc = acc * alpha + jnp.dot(vt_ref[:, ks], p.astype(BF16), preferred_element_type=F32)
        m = m_new
    o_ref[...] = jnp.transpose(acc / l).astype(BF16)


def _attention(qp, kp, vt):
    return pl.pallas_call(
        _attn_kernel,
        grid=(HEADS, S // TQ),
        in_specs=[pl.BlockSpec((TQ, HQ), lambda h, i: (i, h)),
                  pl.BlockSpec((TT, HQ), lambda h, i: (0, h)),
                  pl.BlockSpec((VH, TT), lambda h, i: (h, 0))],
        out_specs=pl.BlockSpec((TQ, VH), lambda h, i: (i, h)),
        out_shape=jax.ShapeDtypeStruct((S, HEADS * VH), BF16),
        compiler_params=_cp(("arbitrary", "arbitrary")),
        name="attention",
    )(qp, kp, vt)


FFT1_BN = 2048


def _fft1_kernel(m_ref, z_ref, y_ref):
    y_ref[...] = jnp.dot(m_ref[...], z_ref[...], preferred_element_type=F32).astype(BF16)


def _fft2_kernel(g_ref, y_ref, o_ref):
    y = y_ref[...].reshape(2 * FFT_N2, FW)
    o_ref[0] = jnp.dot(g_ref[0], y, preferred_element_type=F32).astype(BF16)


def _fft_tables():
    k1 = np.arange(FFT_N1, dtype=np.float64)
    a = 2.0 * np.pi * np.outer(k1, k1) / FFT_N1
    c1, s1 = np.cos(a), np.sin(a)
    m1 = np.block([[c1, s1], [-s1, c1]])
    t2 = np.arange(FFT_N2, dtype=np.float64)
    k2 = np.arange(FFT_N2, dtype=np.float64)
    th = 2.0 * np.pi * (k1[:, None, None] * t2[None, None, :] / (FFT_N1 * FFT_N2)
                        + k2[None, :, None] * t2[None, None, :] / FFT_N2)
    g = np.concatenate([np.cos(th), np.sin(th)], axis=2) / np.sqrt(float(S * FG))
    cc = np.arange(FG, dtype=np.float64)
    ac = 2.0 * np.pi * np.outer(cc, cc) / FG
    dft = np.concatenate([np.cos(ac), -np.sin(ac)], axis=1)
    return (jnp.asarray(m1, F32).astype(BF16), jnp.asarray(g, F32).astype(BF16),
            jnp.asarray(dft, F32).astype(BF16))


def _fourier_tokens(z, m1, gtab):
    ncol = FFT_N2 * FW
    z2 = z[:, :S].reshape(2 * FFT_N1, ncol)
    y = pl.pallas_call(
        _fft1_kernel,
        grid=(ncol // FFT1_BN,),
        in_specs=[_const_spec((2 * FFT_N1, 2 * FFT_N1)),
                  pl.BlockSpec((2 * FFT_N1, FFT1_BN), lambda j: (0, j))],
        out_specs=pl.BlockSpec((2 * FFT_N1, FFT1_BN), lambda j: (0, j)),
        out_shape=jax.ShapeDtypeStruct((2 * FFT_N1, ncol), BF16),
        compiler_params=_cp(("arbitrary",)),
        name="fft1",
    )(m1, z2)
    y4 = y.reshape(2, FFT_N1, FFT_N2, FW)
    o = pl.pallas_call(
        _fft2_kernel,
        grid=(FFT_N1,),
        in_specs=[pl.BlockSpec((1, FFT_N2, 2 * FFT_N2), lambda k: (k, 0, 0)),
                  pl.BlockSpec((2, None, FFT_N2, FW), lambda k: (0, k, 0, 0))],
        out_specs=pl.BlockSpec((1, FFT_N2, FW), lambda k: (k, 0, 0)),
        out_shape=jax.ShapeDtypeStruct((FFT_N1, FFT_N2, FW), BF16),
        compiler_params=_cp(("arbitrary",)),
        name="fft2",
    )(gtab, y4)
    return jnp.transpose(o, (1, 0, 2)).reshape(S, FW)


def _route(h2, rw_ref, rb_ref, ids_ref, wts_ref):
    hi = h2.astype(BF16)
    lo = (h2 - hi.astype(F32)).astype(BF16)
    r = jnp.dot(jnp.concatenate([hi, lo], axis=0), rw_ref[...], preferred_element_type=F32)
    logits = (r[0:TM, 0:128] + r[0:TM, 128:256]) + (r[TM:2 * TM, 0:128] + r[TM:2 * TM, 128:256])
    lt = jnp.transpose(logits)[0:N_E, :]
    sc = jax.nn.sigmoid(lt)
    bi = sc + rb_ref[...]
    sj = [sc[j * N_G:(j + 1) * N_G] for j in range(EPG)]
    bj = [bi[j * N_G:(j + 1) * N_G] for j in range(EPG)]
    m01, n01 = jnp.maximum(bj[0], bj[1]), jnp.minimum(bj[0], bj[1])
    m23, n23 = jnp.maximum(bj[2], bj[3]), jnp.minimum(bj[2], bj[3])
    gs = jnp.maximum(m01, m23) + jnp.maximum(jnp.minimum(m01, m23), jnp.maximum(n01, n23))
    gmax = jnp.max(gs, axis=0, keepdims=True)
    gi = lax.broadcasted_iota(jnp.int32, gs.shape, 0)
    best = jnp.min(jnp.where(gs == gmax, gi, N_G), axis=0, keepdims=True)
    sel = gi == best
    b = [jnp.sum(jnp.where(sel, x, 0.0), axis=0, keepdims=True) for x in bj]
    s = [jnp.sum(jnp.where(sel, x, 0.0), axis=0, keepdims=True) for x in sj]
    b1 = jnp.maximum(jnp.maximum(b[0], b[1]), jnp.maximum(b[2], b[3]))
    j1 = jnp.full(b1.shape, EPG, jnp.int32)
    for j in reversed(range(EPG)):
        j1 = jnp.where(b[j] == b1, j, j1)
    c = [jnp.where(j1 == j, -jnp.inf, b[j]) for j in range(EPG)]
    b2 = jnp.maximum(jnp.maximum(c[0], c[1]), jnp.maximum(c[2], c[3]))
    j2 = jnp.full(b1.shape, EPG, jnp.int32)
    for j in reversed(range(EPG)):
        j2 = jnp.where(c[j] == b2, j, j2)
    s1 = sum(jnp.where(j1 == j, s[j], 0.0) for j in range(EPG))
    s2 = sum(jnp.where(j2 == j, s[j], 0.0) for j in range(EPG))
    tot = s1 + s2
    zi = jnp.zeros((6,) + b1.shape[1:], jnp.int32)
    zf = jnp.zeros((6,) + b1.shape[1:], F32)
    ids_ref[...] = jnp.concatenate([best * EPG + j1, best * EPG + j2, zi], axis=0)
    wts_ref[...] = jnp.concatenate([s1 / tot, s2 / tot, zf], axis=0)


def _tail(y, x_ref, mod_ref, g2_ref, rw_ref, rb_ref, x1_ref, h2_ref, ids_ref, wts_ref):
    mod = mod_ref[0, 0:1, :]
    x1 = x_ref[...] + mod[:, 0:D] * y
    x1_ref[...] = x1
    h2 = _rms(x1) * g2_ref[...] * (1.0 + mod[:, 2 * D:3 * D]) + mod[:, D:2 * D]
    _store_rows(h2_ref, h2, TM)
    _route(h2, rw_ref, rb_ref, ids_ref, wts_ref)


def _mix_tail_kernel(a_ref, f_ref, wa_ref, wf_ref, x_ref, mod_ref, g2_ref, rw_ref, rb_ref,
                     x1_ref, h2_ref, ids_ref, wts_ref):
    y = (jnp.dot(a_ref[...], wa_ref[...], preferred_element_type=F32)
         + jnp.dot(f_ref[...], wf_ref[...], preferred_element_type=F32))
    _tail(y, x_ref, mod_ref, g2_ref, rw_ref, rb_ref, x1_ref, h2_ref, ids_ref, wts_ref)


def _tail_specs():
    row = lambda i: (i, 0)
    in_specs = [pl.BlockSpec((TM, D), row),
                pl.BlockSpec((1, 8, 3 * D), lambda i: (0, 0, 0)),
                _const_spec((1, D)),
                _const_spec((D, 256)),
                _const_spec((N_E, 1))]
    out_specs = [pl.BlockSpec((TM, D), row),
                 pl.BlockSpec((TM * SLABS, 128), row),
                 pl.BlockSpec((8, TM), lambda i: (0, i)),
                 pl.BlockSpec((8, TM), lambda i: (0, i))]
    out_shape = [jax.ShapeDtypeStruct((S, D), F32),
                 jax.ShapeDtypeStruct((S * SLABS, 128), F32),
                 jax.ShapeDtypeStruct((8, S), jnp.int32),
                 jax.ShapeDtypeStruct((8, S), F32)]
    return in_specs, out_specs, out_shape


def _mix_tail(attn, four, wa, wf, x, mod, g2, rw, rb):
    row = lambda i: (i, 0)
    tin, tout, tshape = _tail_specs()
    return pl.pallas_call(
        _mix_tail_kernel,
        grid=(S // TM,),
        in_specs=[pl.BlockSpec((TM, HEADS * VH), row),
                  pl.BlockSpec((TM, FW), row),
                  _const_spec((HEADS * VH, D)),
                  _const_spec((FW, D))] + tin,
        out_specs=tout,
        out_shape=tshape,
        compiler_params=_cp(("arbitrary",)),
        name="mix_tail",
    )(attn, four, wa, wf, x, mod, g2, rw, rb)


def _conv_in_kernel(h_ref, w_ref, b_ref, cz_ref):
    u = jnp.dot(h_ref[...], w_ref[...], preferred_element_type=F32)
    b_ref[...] = u[:, 0:D].astype(BF16)
    cz_ref[...] = (u[:, D:2 * D] * u[:, 2 * D:3 * D]).astype(BF16)


def _conv_in(h, w):
    row = lambda i: (i, 0)
    return pl.pallas_call(
        _conv_in_kernel,
        grid=(S // TM,),
        in_specs=[pl.BlockSpec((TM, D), row), _const_spec((D, 3 * D))],
        out_specs=[pl.BlockSpec((TM, D), row), pl.BlockSpec((TM, D), row)],
        out_shape=[jax.ShapeDtypeStruct((S, D), BF16), jax.ShapeDtypeStruct((S, D), BF16)],
        compiler_params=_cp(("arbitrary",)),
        name="conv_in",
    )(h, w)


def _conv_tail_kernel(b_ref, cz_ref, prev_ref, next_ref, cw_ref, wo_ref, x_ref, mod_ref, g2_ref,
                      rw_ref, rb_ref, x1_ref, h2_ref, ids_ref, wts_ref):
    i = pl.program_id(0)
    cz = cz_ref[...].astype(F32)
    rid = lax.broadcasted_iota(jnp.int32, cz.shape, 0)
    prev_row = jnp.where(i > 0, prev_ref[15:16, :].astype(F32), 0.0)
    next_row = jnp.where(i < S // TM - 1, next_ref[0:1, :].astype(F32), 0.0)
    dn = jnp.where(rid == 0, prev_row, pltpu.roll(cz, 1, 0))
    up = jnp.where(rid == TM - 1, next_row, pltpu.roll(cz, TM - 1, 0))
    y = dn * cw_ref[0:1, :] + cz * cw_ref[1:2, :] + up * cw_ref[2:3, :]
    yb = (b_ref[...].astype(F32) * y).astype(BF16)
    y = jnp.dot(yb, wo_ref[...], preferred_element_type=F32)
    _tail(y, x_ref, mod_ref, g2_ref, rw_ref, rb_ref, x1_ref, h2_ref, ids_ref, wts_ref)


def _conv_tail(bg, cz, cw, wo, x, mod, g2, rw, rb):
    row = lambda i: (i, 0)
    tin, tout, tshape = _tail_specs()
    nb = TM // 16
    return pl.pallas_call(
        _conv_tail_kernel,
        grid=(S // TM,),
        in_specs=[pl.BlockSpec((TM, D), row),
                  pl.BlockSpec((TM, D), row),
                  pl.BlockSpec((16, D), lambda i: (jnp.maximum(i * nb - 1, 0), 0)),
                  pl.BlockSpec((16, D), lambda i: (jnp.minimum((i + 1) * nb, S // 16 - 1), 0)),
                  _const_spec((8, D)),
                  _const_spec((D, D))] + tin,
        out_specs=tout,
        out_shape=tshape,
        compiler_params=_cp(("arbitrary",)),
        name="conv_tail",
    )(bg, cz, cz, cz, cw, wo, x, mod, g2, rw, rb)


def _row_slab(ref, r):
    return ref.at[pl.ds(pl.multiple_of(r * SLABS, SLABS), SLABS)]


def _moe_kernel(te_ref, first_ref, nused_ref, src_ref, nsrc_ref, h_ref, wg_ref, wu_ref, wd_ref, o_ref,
                xbuf, sems, wg_s, wu_s, wd_s):
    i = pl.program_id(0)
    slot = i % 2
    nused = nused_ref[0]

    def dma(s_ref, st, r):
        return pltpu.make_async_copy(_row_slab(h_ref, s_ref[0, r]), _row_slab(xbuf.at[st], r), sems.at[st])

    def issue(s_ref, st):
        def body(r, carry):
            dma(s_ref, st, r).start()
            return carry
        lax.fori_loop(0, MOE_TM, body, 0, unroll=8)

    @pl.when(i == 0)
    def _():
        issue(src_ref, 0)

    @pl.when(i + 1 < nused)
    def _():
        issue(nsrc_ref, 1 - slot)

    @pl.when(first_ref[i] == 1)
    def _():
        wg_s[...] = wg_ref[...].astype(BF16)
        wu_s[...] = wu_ref[...].astype(BF16)
        wd_s[...] = wd_ref[...].astype(BF16)

    @pl.when(i < nused)
    def _():
        def drain(r, carry):
            dma(src_ref, slot, r).wait()
            return carry
        lax.fori_loop(0, MOE_TM, drain, 0, unroll=8)
        x = _load_rows(xbuf.at[slot], MOE_TM).astype(BF16)
        g = jnp.dot(x, wg_s[...], preferred_element_type=F32)
        u = jnp.dot(x, wu_s[...], preferred_element_type=F32)
        a = (_silu(g) * u).astype(BF16)
        _store_rows(o_ref, jnp.dot(a, wd_s[...], preferred_element_type=F32), MOE_TM)

    @pl.when(i >= nused)
    def _():
        o_ref[...] = jnp.zeros(o_ref.shape, F32)


def _moe(te, first, nused, src, h2, w_gate, w_up, w_down, layer):
    wmap = lambda i, te, first, nused: (layer, te[i], 0, 0)
    smem = lambda f: pl.BlockSpec((None, 1, MOE_TM), f, memory_space=pltpu.SMEM)
    grid_spec = pltpu.PrefetchScalarGridSpec(
        num_scalar_prefetch=3,
        grid=(MOE_NT,),
        in_specs=[smem(lambda i, te, first, nused: (i, 0, 0)),
                  smem(lambda i, te, first, nused: (jnp.minimum(i + 1, MOE_NT - 1), 0, 0)),
                  pl.BlockSpec(memory_space=pl.ANY),
                  pl.BlockSpec((None, None, D, D_E), wmap),
                  pl.BlockSpec((None, None, D, D_E), wmap),
                  pl.BlockSpec((None, None, D_E, D), wmap)],
        out_specs=pl.BlockSpec((MOE_TM * SLABS, 128), lambda i, te, first, nused: (i, 0)),
        scratch_shapes=[pltpu.VMEM((2, MOE_TM * SLABS, 128), F32), pltpu.SemaphoreType.DMA((2,)),
                        pltpu.VMEM((D, D_E), BF16), pltpu.VMEM((D, D_E), BF16),
                        pltpu.VMEM((D_E, D), BF16)],
    )
    return pl.pallas_call(
        _moe_kernel,
        grid_spec=grid_spec,
        out_shape=jax.ShapeDtypeStruct((P_ROWS * SLABS, 128), F32),
        compiler_params=_cp(("arbitrary",)),
        name="moe",
    )(te, first, nused, src, src, h2, w_gate, w_up, w_down)


def _route_plan(ids):
    ef = ids[0:2].T.reshape(-1)
    oh = (ef[:, None] == jnp.arange(N_E, dtype=jnp.int32)[None, :]).astype(jnp.int32)
    csum = jnp.cumsum(oh, axis=0)
    rank = jnp.sum(csum * oh, axis=1) - 1
    cnt = csum[-1]
    pc = ((cnt + MOE_TM - 1) // MOE_TM) * MOE_TM
    end = jnp.cumsum(pc)
    off = end - pc
    pos = jnp.sum(oh * off[None, :], axis=1) + rank
    nused = (end[-1] // MOE_TM).astype(jnp.int32)
    ti = jnp.arange(MOE_NT, dtype=jnp.int32)
    te = jnp.sum((ti[:, None] >= (end // MOE_TM)[None, :]).astype(jnp.int32), axis=1)
    te_last = jnp.sum((nused - 1 >= end // MOE_TM).astype(jnp.int32))
    te = jnp.minimum(te, te_last).astype(jnp.int32)
    first = jnp.concatenate([jnp.ones((1,), jnp.int32), (te[1:] != te[:-1]).astype(jnp.int32)])
    pos = pos.astype(jnp.int32)
    tok = jnp.arange(2 * S, dtype=jnp.int32) // 2
    src = jnp.zeros((P_ROWS,), jnp.int32).at[pos].set(tok, unique_indices=True)
    return pos, src, te, first, nused.reshape(1)


def _moe_block(h2, ids, w_gate, w_up, w_down, layer):
    pos, src, te, first, nused = _route_plan(ids)
    ys = _moe(te, first, nused, src.reshape(MOE_NT, 1, MOE_TM), h2, w_gate, w_up, w_down, layer)
    return ys, pos.reshape(S // TM, 1, 2 * TM)


def _combine(pos_ref, npos_ref, ys_ref, x1_ref, w_ref, mod_ref, buf, sems):
    i = pl.program_id(0)
    slot = i % 2

    def dma(p_ref, st, r, k):
        return pltpu.make_async_copy(_row_slab(ys_ref, p_ref[0, 2 * r + k]),
                                     _row_slab(buf.at[st, k], r), sems.at[st])

    def issue(p_ref, st):
        def body(r, carry):
            dma(p_ref, st, r, 0).start()
            dma(p_ref, st, r, 1).start()
            return carry
        lax.fori_loop(0, TM, body, 0, unroll=8)

    @pl.when(i == 0)
    def _():
        issue(pos_ref, 0)

    @pl.when(i + 1 < pl.num_programs(0))
    def _():
        issue(npos_ref, 1 - slot)

    def drain(r, carry):
        dma(pos_ref, slot, r, 0).wait()
        dma(pos_ref, slot, r, 1).wait()
        return carry

    lax.fori_loop(0, TM, drain, 0, unroll=8)
    w = w_ref[...]
    moe = w[:, 0:1] * _load_rows(buf.at[slot, 0], TM) + w[:, 1:2] * _load_rows(buf.at[slot, 1], TM)
    return x1_ref[...] + mod_ref[0, 0:1, :] * moe


def _fuse_next_kernel(pos_ref, npos_ref, ys_ref, x1_ref, w_ref, mod_ref, nmod_ref, g_ref, x2_ref, h_ref,
                      buf, sems):
    x2 = _combine(pos_ref, npos_ref, ys_ref, x1_ref, w_ref, mod_ref, buf, sems)
    x2_ref[...] = x2
    nm = nmod_ref[0, 0:1, :]
    h_ref[...] = (_rms(x2) * g_ref[...] * (1.0 + nm[:, D:2 * D]) + nm[:, 0:D]).astype(BF16)


def _fuse_final_kernel(pos_ref, npos_ref, ys_ref, x1_ref, w_ref, mod_ref, g_ref, o_ref, buf, sems):
    x2 = _combine(pos_ref, npos_ref, ys_ref, x1_ref, w_ref, mod_ref, buf, sems)
    o_ref[...] = _rms(x2) * g_ref[...]


def _fuse_specs():
    row = lambda i: (i, 0)
    last = S // TM - 1
    return [pl.BlockSpec((None, 1, 2 * TM), lambda i: (i, 0, 0), memory_space=pltpu.SMEM),
            pl.BlockSpec((None, 1, 2 * TM), lambda i: (jnp.minimum(i + 1, last), 0, 0),
                         memory_space=pltpu.SMEM),
            pl.BlockSpec(memory_space=pl.ANY),
            pl.BlockSpec((TM, D), row),
            pl.BlockSpec((TM, 8), row),
            pl.BlockSpec((1, 8, D), lambda i: (0, 0, 0))]


def _fuse_scratch():
    return [pltpu.VMEM((2, 2, TM * SLABS, 128), F32), pltpu.SemaphoreType.DMA((2,))]


def _fuse_next(pos, ys, x1, wt, mod_g2, nmod, g):
    row = lambda i: (i, 0)
    return pl.pallas_call(
        _fuse_next_kernel,
        grid=(S // TM,),
        in_specs=_fuse_specs() + [pl.BlockSpec((1, 8, 2 * D), lambda i: (0, 0, 0)),
                                  _const_spec((1, D))],
        out_specs=[pl.BlockSpec((TM, D), row), pl.BlockSpec((TM, D), row)],
        out_shape=[jax.ShapeDtypeStruct((S, D), F32), jax.ShapeDtypeStruct((S, D), BF16)],
        scratch_shapes=_fuse_scratch(),
        compiler_params=_cp(("arbitrary",)),
        name="fuse_next",
    )(pos, pos, ys, x1, wt, mod_g2, nmod, g)


def _fuse_final(pos, ys, x1, wt, mod_g2, g):
    row = lambda i: (i, 0)
    return pl.pallas_call(
        _fuse_final_kernel,
        grid=(S // TM,),
        in_specs=_fuse_specs() + [_const_spec((1, D))],
        out_specs=pl.BlockSpec((TM, D), row),
        out_shape=jax.ShapeDtypeStruct((S, D), F32),
        scratch_shapes=_fuse_scratch(),
        compiler_params=_cp(("arbitrary",)),
        name="fuse_final",
    )(pos, pos, ys, x1, wt, mod_g2, g)


def _rope_tables():
    half = ROPE // 2
    nrow = S // GRID_W
    inv = THETA ** (-jnp.arange(0, half, 2, dtype=F32) / half)
    ang_r = jnp.arange(nrow).astype(F32)[:, None] * inv
    ang_c = jnp.arange(GRID_W).astype(F32)[:, None] * inv
    cr, sr = (jnp.repeat(f(ang_r), GRID_W, axis=0) for f in (jnp.cos, jnp.sin))
    cc, sc = (jnp.tile(f(ang_c), (nrow, 1)) for f in (jnp.cos, jnp.sin))
    one = jnp.ones((S, 64), F32)
    zero = jnp.zeros((S, 64), F32)
    cos_t = jnp.concatenate([cr, cr, cc, cc, one], axis=1)
    sin_t = jnp.concatenate([-sr, sr, -sc, sc, zero], axis=1)
    cos_t = jnp.concatenate([cos_t, jnp.ones((CTX, 128), F32)], axis=0)
    sin_t = jnp.concatenate([sin_t, jnp.zeros((CTX, 128), F32)], axis=0)
    return cos_t, sin_t


def _prep_layer0(attn_in_w, q_up_w, kv_up_w):
    c0, c1, c2 = Q_LORA, Q_LORA + KV_LORA, Q_LORA + KV_LORA + ROPE
    w_in = jnp.concatenate([attn_in_w[:, :c2], jnp.zeros((D, 128 - ROPE), F32), attn_in_w[:, c2:]],
                           axis=1).astype(BF16)
    qw = q_up_w.reshape(Q_LORA, HEADS, NOPE + ROPE)
    wq = jnp.concatenate([qw, jnp.zeros((Q_LORA, HEADS, HQ - NOPE - ROPE), F32)],
                         axis=2).reshape(Q_LORA, HEADS * HQ).astype(BF16)
    kvw = kv_up_w.reshape(KV_LORA, HEADS, NOPE + VH)
    wk = kvw[:, :, :NOPE].reshape(KV_LORA, HEADS * NOPE).astype(BF16)
    wv = kvw[:, :, NOPE:].reshape(KV_LORA, HEADS * VH).astype(BF16)
    del c0, c1
    return w_in, wq, wk, wv


def _prep_router(router_w, router_b):
    perm = np.array([EPG * (s % N_G) + s // N_G for s in range(N_E)])
    rw = jnp.concatenate([router_w[:, perm], jnp.zeros((D, 128 - N_E), F32)], axis=1)
    rw_hi = rw.astype(BF16)
    rw_lo = (rw - rw_hi.astype(F32)).astype(BF16)
    rb = router_b[perm].reshape(N_E, 1)
    return jnp.concatenate([rw_hi, rw_lo], axis=1), rb


def kernel(x, c, ctx, c_ctx, ada_w, ada_b, norm1_g, norm2_g, attn_in_w, q_norm_g, q_up_w, kv_norm_g,
           kv_up_w, conv_in_w, conv_w, mix_out_w, router_w, router_b, moe_w_gate, moe_w_up,
           moe_w_down, final_norm_g):
    x = x.reshape(S, D)
    mods = _ada(c.reshape(D), c_ctx, ada_w, ada_b)
    rw, rb = _prep_router(router_w, router_b)
    m1, gtab, dft = _fft_tables()
    cos_t, sin_t = _rope_tables()

    w_in, wq, wk, wv = _prep_layer0(attn_in_w[0], q_up_w[0], kv_up_w[0])
    xc = jnp.concatenate([x, ctx.reshape(CTX, D)], axis=0)
    qp, kp, vt, z = _inproj(xc, mods[0:1, :, 0:2 * D], norm1_g[0:1], w_in, q_norm_g[0:1], wq,
                           kv_norm_g[0:1], wk, wv, cos_t, sin_t, dft)
    attn = _attention(qp, kp, vt)
    four = _fourier_tokens(z, m1, gtab)
    wo = mix_out_w[0].astype(BF16)
    x1, h2, ids, wts = _mix_tail(attn, four, wo[:HEADS * VH], wo[HEADS * VH:], x,
                                 mods[0:1, :, 2 * D:5 * D], norm2_g[0:1], rw, rb)
    ys, pos = _moe_block(h2, ids, moe_w_gate, moe_w_up, moe_w_down, 0)
    x2, h = _fuse_next(pos, ys, x1, wts.T, mods[0:1, :, 5 * D:6 * D], mods[1:2, :, 0:2 * D],
                       norm1_g[1:2])

    bg, cz = _conv_in(h, conv_in_w[0].astype(BF16))
    cw = jnp.concatenate([conv_w[0], jnp.zeros((5, D), F32)], axis=0)
    x1, h2, ids, wts = _conv_tail(bg, cz, cw, mix_out_w[1].astype(BF16), x2,
                                  mods[1:2, :, 2 * D:5 * D], norm2_g[1:2], rw, rb)
    ys, pos = _moe_block(h2, ids, moe_w_gate, moe_w_up, moe_w_down, 1)
    out = _fuse_final(pos, ys, x1, wts.T, mods[1:2, :, 5 * D:6 * D], final_norm_g.reshape(1, D))
    return out.reshape(1, S, D)
```

```python
import functools

import numpy as np
import jax
import jax.numpy as jnp
from jax import lax
from jax.experimental import pallas as pl
from jax.experimental.pallas import tpu as pltpu

F32 = jnp.float32
BF16 = jnp.bfloat16

D = 2048
S = 8192
CTX = 256
TT = S + CTX
GRID_W = 64
HEADS = 8
Q_LORA = 512
KV_LORA = 512
NOPE = 128
ROPE = 64
VH = 128
HQ = 256
THETA = 10000.0
QSCALE = (NOPE + ROPE) ** -0.5 * float(np.log2(np.e))
FW = 1024
FG = 128
N_E = 32
N_G = 8
EPG = 4
D_E = 512
EPS = 1e-6
NMOD = 6

TM = 256
TQ = 256
MOE_TM = 128
P_ROWS = 2 * S + N_E * MOE_TM
MOE_NT = P_ROWS // MOE_TM
FFT_N1 = 128
FFT_N2 = 64

VMEM_LIMIT = 56 * 1024 * 1024


def _cp(sem, limit=VMEM_LIMIT):
    return pltpu.CompilerParams(dimension_semantics=sem, vmem_limit_bytes=limit)


def _const_spec(shape):
    n = len(shape)
    return pl.BlockSpec(shape, lambda *a: (0,) * n, pipeline_mode=pl.Buffered(1))


SLABS = D // 128


def _load_rows(ref, n):
    return jnp.concatenate([ref[pl.ds(j, n, stride=SLABS), :] for j in range(SLABS)], axis=1)


def _store_rows(ref, val, n):
    for j in range(SLABS):
        ref[pl.ds(j, n, stride=SLABS), :] = val[:, j * 128:(j + 1) * 128]


def _rms(x):
    return x * lax.rsqrt(jnp.mean(x * x, axis=-1, keepdims=True) + EPS)


def _silu(x):
    return x * jax.nn.sigmoid(x)


ADA_BN = 1024


def _ada_kernel(cb_ref, w_ref, b_ref, o_ref):
    a0 = _silu(cb_ref[0])
    a1 = _silu(cb_ref[1])
    rows = []
    for j in range(ADA_BN // 128):
        w = w_ref[0, :, j * 128:(j + 1) * 128]
        r0 = jnp.sum(w * a0, axis=0, keepdims=True)
        r1 = jnp.sum(w * a1, axis=0, keepdims=True)
        rows.append(jnp.concatenate([r0, r1, jnp.zeros((6, 128), F32)], axis=0))
    o_ref[0] = jnp.concatenate(rows, axis=1) + b_ref[0]


def _ada(c, c_ctx, ada_w, ada_b):
    depth = ada_w.shape[0]
    n = ada_w.shape[2]
    cb = jnp.stack([jnp.broadcast_to(c.reshape(D, 1), (D, 128)),
                    jnp.broadcast_to(c_ctx.reshape(D, 1), (D, 128))])
    return pl.pallas_call(
        _ada_kernel,
        grid=(depth, n // ADA_BN),
        in_specs=[pl.BlockSpec((2, D, 128), lambda l, j: (0, 0, 0)),
                  pl.BlockSpec((1, D, ADA_BN), lambda l, j: (l, 0, j)),
                  pl.BlockSpec((1, 1, ADA_BN), lambda l, j: (l, 0, j))],
        out_specs=pl.BlockSpec((1, 8, ADA_BN), lambda l, j: (l, 0, j)),
        out_shape=jax.ShapeDtypeStruct((depth, 8, n), F32),
        compiler_params=_cp(("arbitrary", "arbitrary")),
        name="ada",
    )(cb, ada_w, ada_b.reshape(depth, 1, n))


W_IN_COLS = Q_LORA + KV_LORA + 128 + FW


def _rope_chunk(x, cos, sin):
    lane = lax.broadcasted_iota(jnp.int32, x.shape, 1)
    sw = jnp.where((lane % 32) < 16, pltpu.roll(x, 112, 1), pltpu.roll(x, 16, 1))
    return x * cos + sw * sin


def _inproj_kernel(x_ref, mod_ref, g_ref, win_ref, qg_ref, wq_ref, kg_ref, wk_ref, wv_ref,
                   cos_ref, sin_ref, dft_ref, q_out, k_out, v_out, z_out):
    i = pl.program_id(0)
    is_ctx = i >= S // TM
    mod = jnp.where(is_ctx, mod_ref[0, 1:2, :], mod_ref[0, 0:1, :])
    h = _rms(x_ref[...]) * g_ref[...] * (1.0 + mod[:, D:2 * D]) + mod[:, 0:D]
    u = jnp.dot(h.astype(BF16), win_ref[...], preferred_element_type=F32)
    cos = cos_ref[...]
    sin = sin_ref[...]
    qc = (_rms(u[:, 0:Q_LORA]) * qg_ref[...]).astype(BF16)
    q = jnp.dot(qc, wq_ref[...], preferred_element_type=F32)
    for hd in range(HEADS):
        q_out[:, hd * HQ:hd * HQ + NOPE] = (q[:, hd * HQ:hd * HQ + NOPE] * QSCALE).astype(BF16)
        qr = _rope_chunk(q[:, hd * HQ + NOPE:(hd + 1) * HQ], cos, sin) * QSCALE
        q_out[:, hd * HQ + NOPE:(hd + 1) * HQ] = qr.astype(BF16)
    kc = (_rms(u[:, Q_LORA:Q_LORA + KV_LORA]) * kg_ref[...]).astype(BF16)
    kn = jnp.dot(kc, wk_ref[...], preferred_element_type=F32)
    v = jnp.dot(kc, wv_ref[...], preferred_element_type=F32)
    v_out[...] = jnp.transpose(v).astype(BF16)
    kr = _rope_chunk(u[:, 2 * Q_LORA:2 * Q_LORA + 128], cos, sin).astype(BF16)
    for hd in range(HEADS):
        k_out[:, hd * HQ:hd * HQ + NOPE] = kn[:, hd * NOPE:(hd + 1) * NOPE].astype(BF16)
        k_out[:, hd * HQ + NOPE:(hd + 1) * HQ] = kr
    f0 = 2 * Q_LORA + 128
    for g in range(FW // FG):
        fg = u[:, f0 + g * FG:f0 + (g + 1) * FG].astype(BF16)
        zz = jnp.dot(fg, dft_ref[...], preferred_element_type=F32)
        z_out[0, :, g * FG:(g + 1) * FG] = zz[:, 0:FG].astype(BF16)
        z_out[1, :, g * FG:(g + 1) * FG] = zz[:, FG:2 * FG].astype(BF16)


def _inproj(xc, mods, g1, w_in, qg, wq, kg, wk, wv, cos_t, sin_t, dft):
    nt = TT // TM
    row = lambda i: (i, 0)
    return pl.pallas_call(
        _inproj_kernel,
        grid=(nt,),
        in_specs=[pl.BlockSpec((TM, D), row),
                  pl.BlockSpec((1, 8, 2 * D), lambda i: (0, 0, 0)),
                  _const_spec((1, D)),
                  _const_spec((D, W_IN_COLS)),
                  _const_spec((1, Q_LORA)),
                  _const_spec((Q_LORA, HEADS * HQ)),
                  _const_spec((1, KV_LORA)),
                  _const_spec((KV_LORA, HEADS * NOPE)),
                  _const_spec((KV_LORA, HEADS * VH)),
                  pl.BlockSpec((TM, 128), row),
                  pl.BlockSpec((TM, 128), row),
                  _const_spec((FG, 2 * FG))],
        out_specs=[pl.BlockSpec((TM, HEADS * HQ), row),
                   pl.BlockSpec((TM, HEADS * HQ), row),
                   pl.BlockSpec((HEADS * VH, TM), lambda i: (0, i)),
                   pl.BlockSpec((2, TM, FW), lambda i: (0, i, 0))],
        out_shape=[jax.ShapeDtypeStruct((TT, HEADS * HQ), BF16),
                   jax.ShapeDtypeStruct((TT, HEADS * HQ), BF16),
                   jax.ShapeDtypeStruct((HEADS * VH, TT), BF16),
                   jax.ShapeDtypeStruct((2, TT, FW), BF16)],
        compiler_params=_cp(("arbitrary",)),
        name="inproj",
    )(xc, mods, g1, w_in, qg, wq, kg, wk, wv, cos_t, sin_t, dft)


ATT_CK = 2816


def _attn_kernel(q_ref, k_ref, vt_ref, o_ref):
    q = q_ref[...]
    m = jnp.full((1, TQ), -jnp.inf, F32)
    l = jnp.zeros((1, TQ), F32)
    acc = jnp.zeros((VH, TQ), F32)
    nck = TT // ATT_CK

    def scores(c):
        return lax.dot_general(k_ref[c * ATT_CK:(c + 1) * ATT_CK, :], q, (((1,), (1,)), ((), ())),
                               preferred_element_type=F32)

    s_next = scores(0)
    for c in range(nck):
        ks = slice(c * ATT_CK, (c + 1) * ATT_CK)
        s = s_next
        if c + 1 < nck:
            s_next = scores(c + 1)
        m_new = jnp.maximum(m, jnp.max(s, axis=0, keepdims=True))
        alpha = jnp.exp2(m - m_new)
        p = jnp.exp2(s - m_new)
        l = l * alpha + jnp.sum(p, axis=0, keepdims=True)
        acc = acc * alpha + jnp.dot(vt_ref[:, ks], p.astype(BF16), preferred_element_type=F32)
        m = m_new
    o_ref[...] = jnp.transpose(acc / l).astype(BF16)


def _attention(qp, kp, vt):
    return pl.pallas_call(
        _attn_kernel,
        grid=(HEADS, S // TQ),
        in_specs=[pl.BlockSpec((TQ, HQ), lambda h, i: (i, h)),
                  pl.BlockSpec((TT, HQ), lambda h, i: (0, h)),
                  pl.BlockSpec((VH, TT), lambda h, i: (h, 0))],
        out_specs=pl.BlockSpec((TQ, VH), lambda h, i: (i, h)),
        out_shape=jax.ShapeDtypeStruct((S, HEADS * VH), BF16),
        compiler_params=_cp(("arbitrary", "arbitrary")),
        name="attention",
    )(qp, kp, vt)


FFT1_BN = 2048


def _fft1_kernel(m_ref, z_ref, y_ref):
    y_ref[...] = jnp.dot(m_ref[...], z_ref[...], preferred_element_type=F32).astype(BF16)


def _fft2_kernel(g_ref, y_ref, o_ref):
    y = y_ref[...].reshape(2 * FFT_N2, FW)
    o_ref[0] = jnp.dot(g_ref[0], y, preferred_element_type=F32).astype(BF16)


def _fft_tables():
    k1 = np.arange(FFT_N1, dtype=np.float64)
    a = 2.0 * np.pi * np.outer(k1, k1) / FFT_N1
    c1, s1 = np.cos(a), np.sin(a)
    m1 = np.block([[c1, s1], [-s1, c1]])
    t2 = np.arange(FFT_N2, dtype=np.float64)
    k2 = np.arange(FFT_N2, dtype=np.float64)
    th = 2.0 * np.pi * (k1[:, None, None] * t2[None, None, :] / (FFT_N1 * FFT_N2)
                        + k2[None, :, None] * t2[None, None, :] / FFT_N2)
    g = np.concatenate([np.cos(th), np.sin(th)], axis=2) / np.sqrt(float(S * FG))
    cc = np.arange(FG, dtype=np.float64)
    ac = 2.0 * np.pi * np.outer(cc, cc) / FG
    dft = np.concatenate([np.cos(ac), -np.sin(ac)], axis=1)
    return (jnp.asarray(m1, F32).astype(BF16), jnp.asarray(g, F32).astype(BF16),
            jnp.asarray(dft, F32).astype(BF16))


def _fourier_tokens(z, m1, gtab):
    ncol = FFT_N2 * FW
    z2 = z[:, :S].reshape(2 * FFT_N1, ncol)
    y = pl.pallas_call(
        _fft1_kernel,
        grid=(ncol // FFT1_BN,),
        in_specs=[_const_spec((2 * FFT_N1, 2 * FFT_N1)),
                  pl.BlockSpec((2 * FFT_N1, FFT1_BN), lambda j: (0, j))],
        out_specs=pl.BlockSpec((2 * FFT_N1, FFT1_BN), lambda j: (0, j)),
        out_shape=jax.ShapeDtypeStruct((2 * FFT_N1, ncol), BF16),
        compiler_params=_cp(("arbitrary",)),
        name="fft1",
    )(m1, z2)
    y4 = y.reshape(2, FFT_N1, FFT_N2, FW)
    o = pl.pallas_call(
        _fft2_kernel,
        grid=(FFT_N1,),
        in_specs=[pl.BlockSpec((1, FFT_N2, 2 * FFT_N2), lambda k: (k, 0, 0)),
                  pl.BlockSpec((2, None, FFT_N2, FW), lambda k: (0, k, 0, 0))],
        out_specs=pl.BlockSpec((1, FFT_N2, FW), lambda k: (k, 0, 0)),
        out_shape=jax.ShapeDtypeStruct((FFT_N1, FFT_N2, FW), BF16),
        compiler_params=_cp(("arbitrary",)),
        name="fft2",
    )(gtab, y4)
    return jnp.transpose(o, (1, 0, 2)).reshape(S, FW)


def _route(h2, rw_ref, rb_ref, ids_ref, wts_ref):
    hi = h2.astype(BF16)
    lo = (h2 - hi.astype(F32)).astype(BF16)
    r = jnp.dot(jnp.concatenate([hi, lo], axis=0), rw_ref[...], preferred_element_type=F32)
    logits = (r[0:TM, 0:128] + r[0:TM, 128:256]) + (r[TM:2 * TM, 0:128] + r[TM:2 * TM, 128:256])
    lt = jnp.transpose(logits)[0:N_E, :]
    sc = jax.nn.sigmoid(lt)
    bi = sc + rb_ref[...]
    sj = [sc[j * N_G:(j + 1) * N_G] for j in range(EPG)]
    bj = [bi[j * N_G:(j + 1) * N_G] for j in range(EPG)]
    m01, n01 = jnp.maximum(bj[0], bj[1]), jnp.minimum(bj[0], bj[1])
    m23, n23 = jnp.maximum(bj[2], bj[3]), jnp.minimum(bj[2], bj[3])
    gs = jnp.maximum(m01, m23) + jnp.maximum(jnp.minimum(m01, m23), jnp.maximum(n01, n23))
    gmax = jnp.max(gs, axis=0, keepdims=True)
    gi = lax.broadcasted_iota(jnp.int32, gs.shape, 0)
    best = jnp.min(jnp.where(gs == gmax, gi, N_G), axis=0, keepdims=True)
    sel = gi == best
    b = [jnp.sum(jnp.where(sel, x, 0.0), axis=0, keepdims=True) for x in bj]
    s = [jnp.sum(jnp.where(sel, x, 0.0), axis=0, keepdims=True) for x in sj]
    b1 = jnp.maximum(jnp.maximum(b[0], b[1]), jnp.maximum(b[2], b[3]))
    j1 = jnp.full(b1.shape, EPG, jnp.int32)
    for j in reversed(range(EPG)):
        j1 = jnp.where(b[j] == b1, j, j1)
    c = [jnp.where(j1 == j, -jnp.inf, b[j]) for j in range(EPG)]
    b2 = jnp.maximum(jnp.maximum(c[0], c[1]), jnp.maximum(c[2], c[3]))
    j2 = jnp.full(b1.shape, EPG, jnp.int32)
    for j in reversed(range(EPG)):
        j2 = jnp.where(c[j] == b2, j, j2)
    s1 = sum(jnp.where(j1 == j, s[j], 0.0) for j in range(EPG))
    s2 = sum(jnp.where(j2 == j, s[j], 0.0) for j in range(EPG))
    tot = s1 + s2
    zi = jnp.zeros((6,) + b1.shape[1:], jnp.int32)
    zf = jnp.zeros((6,) + b1.shape[1:], F32)
    ids_ref[...] = jnp.concatenate([best * EPG + j1, best * EPG + j2, zi], axis=0)
    wts_ref[...] = jnp.concatenate([s1 / tot, s2 / tot, zf], axis=0)


def _tail(y, x_ref, mod_ref, g2_ref, rw_ref, rb_ref, x1_ref, h2_ref, ids_ref, wts_ref):
    mod = mod_ref[0, 0:1, :]
    x1 = x_ref[...] + mod[:, 0:D] * y
    x1_ref[...] = x1
    h2 = _rms(x1) * g2_ref[...] * (1.0 + mod[:, 2 * D:3 * D]) + mod[:, D:2 * D]
    _store_rows(h2_ref, h2, TM)
    _route(h2, rw_ref, rb_ref, ids_ref, wts_ref)


def _mix_tail_kernel(a_ref, f_ref, wa_ref, wf_ref, x_ref, mod_ref, g2_ref, rw_ref, rb_ref,
                     x1_ref, h2_ref, ids_ref, wts_ref):
    y = (jnp.dot(a_ref[...], wa_ref[...], preferred_element_type=F32)
         + jnp.dot(f_ref[...], wf_ref[...], preferred_element_type=F32))
    _tail(y, x_ref, mod_ref, g2_ref, rw_ref, rb_ref, x1_ref, h2_ref, ids_ref, wts_ref)


def _tail_specs():
    row = lambda i: (i, 0)
    in_specs = [pl.BlockSpec((TM, D), row),
                pl.BlockSpec((1, 8, 3 * D), lambda i: (0, 0, 0)),
                _const_spec((1, D)),
                _const_spec((D, 256)),
                _const_spec((N_E, 1))]
    out_specs = [pl.BlockSpec((TM, D), row),
                 pl.BlockSpec((TM * SLABS, 128), row),
                 pl.BlockSpec((8, TM), lambda i: (0, i)),
                 pl.BlockSpec((8, TM), lambda i: (0, i))]
    out_shape = [jax.ShapeDtypeStruct((S, D), F32),
                 jax.ShapeDtypeStruct((S * SLABS, 128), F32),
                 jax.ShapeDtypeStruct((8, S), jnp.int32),
                 jax.ShapeDtypeStruct((8, S), F32)]
    return in_specs, out_specs, out_shape


def _mix_tail(attn, four, wa, wf, x, mod, g2, rw, rb):
    row = lambda i: (i, 0)
    tin, tout, tshape = _tail_specs()
    return pl.pallas_call(
        _mix_tail_kernel,
        grid=(S // TM,),
        in_specs=[pl.BlockSpec((TM, HEADS * VH), row),
                  pl.BlockSpec((TM, FW), row),
                  _const_spec((HEADS * VH, D)),
                  _const_spec((FW, D))] + tin,
        out_specs=tout,
        out_shape=tshape,
        compiler_params=_cp(("arbitrary",)),
        name="mix_tail",
    )(attn, four, wa, wf, x, mod, g2, rw, rb)


def _conv_in_kernel(h_ref, w_ref, b_ref, cz_ref):
    u = jnp.dot(h_ref[...], w_ref[...], preferred_element_type=F32)
    b_ref[...] = u[:, 0:D].astype(BF16)
    cz_ref[...] = (u[:, D:2 * D] * u[:, 2 * D:3 * D]).astype(BF16)


def _conv_in(h, w):
    row = lambda i: (i, 0)
    return pl.pallas_call(
        _conv_in_kernel,
        grid=(S // TM,),
        in_specs=[pl.BlockSpec((TM, D), row), _const_spec((D, 3 * D))],
        out_specs=[pl.BlockSpec((TM, D), row), pl.BlockSpec((TM, D), row)],
        out_shape=[jax.ShapeDtypeStruct((S, D), BF16), jax.ShapeDtypeStruct((S, D), BF16)],
        compiler_params=_cp(("arbitrary",)),
        name="conv_in",
    )(h, w)


def _conv_tail_kernel(b_ref, cz_ref, prev_ref, next_ref, cw_ref, wo_ref, x_ref, mod_ref, g2_ref,
                      rw_ref, rb_ref, x1_ref, h2_ref, ids_ref, wts_ref):
    i = pl.program_id(0)
    cz = cz_ref[...].astype(F32)
    rid = lax.broadcasted_iota(jnp.int32, cz.shape, 0)
    prev_row = jnp.where(i > 0, prev_ref[15:16, :].astype(F32), 0.0)
    next_row = jnp.where(i < S // TM - 1, next_ref[0:1, :].astype(F32), 0.0)
    dn = jnp.where(rid == 0, prev_row, pltpu.roll(cz, 1, 0))
    up = jnp.where(rid == TM - 1, next_row, pltpu.roll(cz, TM - 1, 0))
    y = dn * cw_ref[0:1, :] + cz * cw_ref[1:2, :] + up * cw_ref[2:3, :]
    yb = (b_ref[...].astype(F32) * y).astype(BF16)
    y = jnp.dot(yb, wo_ref[...], preferred_element_type=F32)
    _tail(y, x_ref, mod_ref, g2_ref, rw_ref, rb_ref, x1_ref, h2_ref, ids_ref, wts_ref)


def _conv_tail(bg, cz, cw, wo, x, mod, g2, rw, rb):
    row = lambda i: (i, 0)
    tin, tout, tshape = _tail_specs()
    nb = TM // 16
    return pl.pallas_call(
        _conv_tail_kernel,
        grid=(S // TM,),
        in_specs=[pl.BlockSpec((TM, D), row),
                  pl.BlockSpec((TM, D), row),
                  pl.BlockSpec((16, D), lambda i: (jnp.maximum(i * nb - 1, 0), 0)),
                  pl.BlockSpec((16, D), lambda i: (jnp.minimum((i + 1) * nb, S // 16 - 1), 0)),
                  _const_spec((8, D)),
                  _const_spec((D, D))] + tin,
        out_specs=tout,
        out_shape=tshape,
        compiler_params=_cp(("arbitrary",)),
        name="conv_tail",
    )(bg, cz, cz, cz, cw, wo, x, mod, g2, rw, rb)


def _row_slab(ref, r):
    return ref.at[pl.ds(pl.multiple_of(r * SLABS, SLABS), SLABS)]


def _moe_kernel(layer, te_ref, first_ref, nxt_ref, nused_ref, src_ref, nsrc_ref, h_ref, wg_hbm, wu_hbm,
                wd_hbm, o_ref, xbuf, xsems, wg_f, wu_f, wd_f, wsems, wg_s, wu_s, wd_s):
    i = pl.program_id(0)
    slot = i % 2
    nused = nused_ref[0]
    last = pl.num_programs(0) - 1

    def xdma(s_ref, st, r):
        return pltpu.make_async_copy(_row_slab(h_ref, s_ref[0, r]), _row_slab(xbuf.at[st], r), xsems.at[st])

    def xloop(fn):
        def body(r, carry):
            fn(r)
            return carry
        lax.fori_loop(0, MOE_TM, body, 0, unroll=8)

    def wdma(e):
        return [pltpu.make_async_copy(w.at[layer, e], buf, wsems.at[k])
                for k, (w, buf) in enumerate(((wg_hbm, wg_f), (wu_hbm, wu_f), (wd_hbm, wd_f)))]

    @pl.when(i == 0)
    def _():
        for c in wdma(te_ref[0]):
            c.start()
        xloop(lambda r: xdma(src_ref, 0, r).start())

    xloop(lambda r: xdma(src_ref, slot, r).wait())

    @pl.when(first_ref[i] == 1)
    def _():
        for c in wdma(te_ref[i]):
            c.wait()
        wg_s[...] = wg_f[...].astype(BF16)
        wu_s[...] = wu_f[...].astype(BF16)
        wd_s[...] = wd_f[...].astype(BF16)

        @pl.when(nxt_ref[i] != te_ref[i])
        def _():
            for c in wdma(nxt_ref[i]):
                c.start()

    @pl.when(i < nused)
    def _():
        x = _load_rows(xbuf.at[slot], MOE_TM).astype(BF16)
        for r in range(MOE_TM):
            xdma(nsrc_ref, 1 - slot, r).start()
        g = jnp.dot(x, wg_s[...], preferred_element_type=F32)
        u = jnp.dot(x, wu_s[...], preferred_element_type=F32)
        a = (_silu(g) * u).astype(BF16)
        _store_rows(o_ref, jnp.dot(a, wd_s[...], preferred_element_type=F32), MOE_TM)

    @pl.when(i >= nused)
    def _():
        xloop(lambda r: xdma(nsrc_ref, 1 - slot, r).start())
        o_ref[...] = jnp.zeros(o_ref.shape, F32)

    @pl.when(i == last)
    def _():
        xloop(lambda r: xdma(nsrc_ref, 1 - slot, r).wait())


def _moe(te, first, nxt, nused, src, h2, w_gate, w_up, w_down, layer):
    smem = lambda f: pl.BlockSpec((None, 1, MOE_TM), f, memory_space=pltpu.SMEM)
    hbm = pl.BlockSpec(memory_space=pl.ANY)
    grid_spec = pltpu.PrefetchScalarGridSpec(
        num_scalar_prefetch=4,
        grid=(MOE_NT,),
        in_specs=[smem(lambda i, *_: (i, 0, 0)),
                  smem(lambda i, *_: (jnp.minimum(i + 1, MOE_NT - 1), 0, 0)),
                  hbm, hbm, hbm, hbm],
        out_specs=pl.BlockSpec((MOE_TM * SLABS, 128), lambda i, *_: (i, 0)),
        scratch_shapes=[pltpu.VMEM((2, MOE_TM * SLABS, 128), F32), pltpu.SemaphoreType.DMA((2,)),
                        pltpu.VMEM((D, D_E), F32), pltpu.VMEM((D, D_E), F32), pltpu.VMEM((D_E, D), F32),
                        pltpu.SemaphoreType.DMA((3,)),
                        pltpu.VMEM((D, D_E), BF16), pltpu.VMEM((D, D_E), BF16),
                        pltpu.VMEM((D_E, D), BF16)],
    )
    return pl.pallas_call(
        functools.partial(_moe_kernel, layer),
        grid_spec=grid_spec,
        out_shape=jax.ShapeDtypeStruct((P_ROWS * SLABS, 128), F32),
        compiler_params=_cp(("arbitrary",)),
        name="moe",
    )(te, first, nxt, nused, src, src, h2, w_gate, w_up, w_down)


def _route_plan(ids):
    ef = ids[0:2].T.reshape(-1)
    oh = (ef[:, None] == jnp.arange(N_E, dtype=jnp.int32)[None, :]).astype(jnp.int32)
    csum = jnp.cumsum(oh, axis=0)
    rank = jnp.sum(csum * oh, axis=1) - 1
    cnt = csum[-1]
    pc = ((cnt + MOE_TM - 1) // MOE_TM) * MOE_TM
    end = jnp.cumsum(pc)
    off = end - pc
    pos = jnp.sum(oh * off[None, :], axis=1) + rank
    nused = (end[-1] // MOE_TM).astype(jnp.int32)
    ti = jnp.arange(MOE_NT, dtype=jnp.int32)
    te = jnp.sum((ti[:, None] >= (end // MOE_TM)[None, :]).astype(jnp.int32), axis=1)
    te_last = jnp.sum((nused - 1 >= end // MOE_TM).astype(jnp.int32))
    te = jnp.minimum(te, te_last).astype(jnp.int32)
    first = jnp.concatenate([jnp.ones((1,), jnp.int32), (te[1:] != te[:-1]).astype(jnp.int32)])
    ee = jnp.arange(N_E, dtype=jnp.int32)
    later = (cnt > 0)[None, :] & (ee[None, :] > te[:, None])
    nxt = jnp.min(jnp.where(later, ee[None, :], N_E), axis=1)
    nxt = jnp.where(nxt == N_E, te, nxt).astype(jnp.int32)
    pos = pos.astype(jnp.int32)
    tok = jnp.arange(2 * S, dtype=jnp.int32) // 2
    src = jnp.zeros((P_ROWS,), jnp.int32).at[pos].set(tok, unique_indices=True)
    return pos, src, te, first, nxt, nused.reshape(1)


def _moe_block(h2, ids, w_gate, w_up, w_down, layer):
    pos, src, te, first, nxt, nused = _route_plan(ids)
    ys = _moe(te, first, nxt, nused, src.reshape(MOE_NT, 1, MOE_TM), h2, w_gate, w_up, w_down, layer)
    return ys, pos.reshape(S // TM, 1, 2 * TM)


def _combine(pos_ref, npos_ref, ys_ref, x1_ref, w_ref, mod_ref, buf, sems):
    i = pl.program_id(0)
    slot = i % 2

    def dma(p_ref, st, r, k):
        return pltpu.make_async_copy(_row_slab(ys_ref, p_ref[0, 2 * r + k]),
                                     _row_slab(buf.at[st, k], r), sems.at[st])

    def issue(p_ref, st):
        def body(r, carry):
            dma(p_ref, st, r, 0).start()
            dma(p_ref, st, r, 1).start()
            return carry
        lax.fori_loop(0, TM, body, 0, unroll=8)

    @pl.when(i == 0)
    def _():
        issue(pos_ref, 0)

    @pl.when(i + 1 < pl.num_programs(0))
    def _():
        issue(npos_ref, 1 - slot)

    def drain(r, carry):
        dma(pos_ref, slot, r, 0).wait()
        dma(pos_ref, slot, r, 1).wait()
        return carry

    lax.fori_loop(0, TM, drain, 0, unroll=8)
    w = w_ref[...]
    moe = w[:, 0:1] * _load_rows(buf.at[slot, 0], TM) + w[:, 1:2] * _load_rows(buf.at[slot, 1], TM)
    return x1_ref[...] + mod_ref[0, 0:1, :] * moe


def _fuse_next_kernel(pos_ref, npos_ref, ys_ref, x1_ref, w_ref, mod_ref, nmod_ref, g_ref, x2_ref, h_ref,
                      buf, sems):
    x2 = _combine(pos_ref, npos_ref, ys_ref, x1_ref, w_ref, mod_ref, buf, sems)
    x2_ref[...] = x2
    nm = nmod_ref[0, 0:1, :]
    h_ref[...] = (_rms(x2) * g_ref[...] * (1.0 + nm[:, D:2 * D]) + nm[:, 0:D]).astype(BF16)


def _fuse_final_kernel(pos_ref, npos_ref, ys_ref, x1_ref, w_ref, mod_ref, g_ref, o_ref, buf, sems):
    x2 = _combine(pos_ref, npos_ref, ys_ref, x1_ref, w_ref, mod_ref, buf, sems)
    o_ref[...] = _rms(x2) * g_ref[...]


def _fuse_specs():
    row = lambda i: (i, 0)
    last = S // TM - 1
    return [pl.BlockSpec((None, 1, 2 * TM), lambda i: (i, 0, 0), memory_space=pltpu.SMEM),
            pl.BlockSpec((None, 1, 2 * TM), lambda i: (jnp.minimum(i + 1, last), 0, 0),
                         memory_space=pltpu.SMEM),
            pl.BlockSpec(memory_space=pl.ANY),
            pl.BlockSpec((TM, D), row),
            pl.BlockSpec((TM, 8), row),
            pl.BlockSpec((1, 8, D), lambda i: (0, 0, 0))]


def _fuse_scratch():
    return [pltpu.VMEM((2, 2, TM * SLABS, 128), F32), pltpu.SemaphoreType.DMA((2,))]


def _fuse_next(pos, ys, x1, wt, mod_g2, nmod, g):
    row = lambda i: (i, 0)
    return pl.pallas_call(
        _fuse_next_kernel,
        grid=(S // TM,),
        in_specs=_fuse_specs() + [pl.BlockSpec((1, 8, 2 * D), lambda i: (0, 0, 0)),
                                  _const_spec((1, D))],
        out_specs=[pl.BlockSpec((TM, D), row), pl.BlockSpec((TM, D), row)],
        out_shape=[jax.ShapeDtypeStruct((S, D), F32), jax.ShapeDtypeStruct((S, D), BF16)],
        scratch_shapes=_fuse_scratch(),
        compiler_params=_cp(("arbitrary",)),
        name="fuse_next",
    )(pos, pos, ys, x1, wt, mod_g2, nmod, g)


def _fuse_final(pos, ys, x1, wt, mod_g2, g):
    row = lambda i: (i, 0)
    return pl.pallas_call(
        _fuse_final_kernel,
        grid=(S // TM,),
        in_specs=_fuse_specs() + [_const_spec((1, D))],
        out_specs=pl.BlockSpec((TM, D), row),
        out_shape=jax.ShapeDtypeStruct((S, D), F32),
        scratch_shapes=_fuse_scratch(),
        compiler_params=_cp(("arbitrary",)),
        name="fuse_final",
    )(pos, pos, ys, x1, wt, mod_g2, g)


def _rope_tables():
    half = ROPE // 2
    nrow = S // GRID_W
    inv = THETA ** (-jnp.arange(0, half, 2, dtype=F32) / half)
    ang_r = jnp.arange(nrow).astype(F32)[:, None] * inv
    ang_c = jnp.arange(GRID_W).astype(F32)[:, None] * inv
    cr, sr = (jnp.repeat(f(ang_r), GRID_W, axis=0) for f in (jnp.cos, jnp.sin))
    cc, sc = (jnp.tile(f(ang_c), (nrow, 1)) for f in (jnp.cos, jnp.sin))
    one = jnp.ones((S, 64), F32)
    zero = jnp.zeros((S, 64), F32)
    cos_t = jnp.concatenate([cr, cr, cc, cc, one], axis=1)
    sin_t = jnp.concatenate([-sr, sr, -sc, sc, zero], axis=1)
    cos_t = jnp.concatenate([cos_t, jnp.ones((CTX, 128), F32)], axis=0)
    sin_t = jnp.concatenate([sin_t, jnp.zeros((CTX, 128), F32)], axis=0)
    return cos_t, sin_t


def _prep_layer0(attn_in_w, q_up_w, kv_up_w):
    c0, c1, c2 = Q_LORA, Q_LORA + KV_LORA, Q_LORA + KV_LORA + ROPE
    w_in = jnp.concatenate([attn_in_w[:, :c2], jnp.zeros((D, 128 - ROPE), F32), attn_in_w[:, c2:]],
                           axis=1).astype(BF16)
    qw = q_up_w.reshape(Q_LORA, HEADS, NOPE + ROPE)
    wq = jnp.concatenate([qw, jnp.zeros((Q_LORA, HEADS, HQ - NOPE - ROPE), F32)],
                         axis=2).reshape(Q_LORA, HEADS * HQ).astype(BF16)
    kvw = kv_up_w.reshape(KV_LORA, HEADS, NOPE + VH)
    wk = kvw[:, :, :NOPE].reshape(KV_LORA, HEADS * NOPE).astype(BF16)
    wv = kvw[:, :, NOPE:].reshape(KV_LORA, HEADS * VH).astype(BF16)
    del c0, c1
    return w_in, wq, wk, wv


def _prep_router(router_w, router_b):
    perm = np.array([EPG * (s % N_G) + s // N_G for s in range(N_E)])
    rw = jnp.concatenate([router_w[:, perm], jnp.zeros((D, 128 - N_E), F32)], axis=1)
    rw_hi = rw.astype(BF16)
    rw_lo = (rw - rw_hi.astype(F32)).astype(BF16)
    rb = router_b[perm].reshape(N_E, 1)
    return jnp.concatenate([rw_hi, rw_lo], axis=1), rb


def kernel(x, c, ctx, c_ctx, ada_w, ada_b, norm1_g, norm2_g, attn_in_w, q_norm_g, q_up_w, kv_norm_g,
           kv_up_w, conv_in_w, conv_w, mix_out_w, router_w, router_b, moe_w_gate, moe_w_up,
           moe_w_down, final_norm_g):
    x = x.reshape(S, D)
    mods = _ada(c.reshape(D), c_ctx, ada_w, ada_b)
    rw, rb = _prep_router(router_w, router_b)
    m1, gtab, dft = _fft_tables()
    cos_t, sin_t = _rope_tables()

    w_in, wq, wk, wv = _prep_layer0(attn_in_w[0], q_up_w[0], kv_up_w[0])
    xc = jnp.concatenate([x, ctx.reshape(CTX, D)], axis=0)
    qp, kp, vt, z = _inproj(xc, mods[0:1, :, 0:2 * D], norm1_g[0:1], w_in, q_norm_g[0:1], wq,
                           kv_norm_g[0:1], wk, wv, cos_t, sin_t, dft)
    attn = _attention(qp, kp, vt)
    four = _fourier_tokens(z, m1, gtab)
    wo = mix_out_w[0].astype(BF16)
    x1, h2, ids, wts = _mix_tail(attn, four, wo[:HEADS * VH], wo[HEADS * VH:], x,
                                 mods[0:1, :, 2 * D:5 * D], norm2_g[0:1], rw, rb)
    ys, pos = _moe_block(h2, ids, moe_w_gate, moe_w_up, moe_w_down, 0)
    x2, h = _fuse_next(pos, ys, x1, wts.T, mods[0:1, :, 5 * D:6 * D], mods[1:2, :, 0:2 * D],
                       norm1_g[1:2])

    bg, cz = _conv_in(h, conv_in_w[0].astype(BF16))
    cw = jnp.concatenate([conv_w[0], jnp.zeros((5, D), F32)], axis=0)
    x1, h2, ids, wts = _conv_tail(bg, cz, cw, mix_out_w[1].astype(BF16), x2,
                                  mods[1:2, :, 2 * D:5 * D], norm2_g[1:2], rw, rb)
    ys, pos = _moe_block(h2, ids, moe_w_gate, moe_w_up, moe_w_down, 1)
    out = _fuse_final(pos, ys, x1, wts.T, mods[1:2, :, 5 * D:6 * D], final_norm_g.reshape(1, D))
    return out.reshape(1, S, D)
```

```python
import functools

import numpy as np
import jax
import jax.numpy as jnp
from jax import lax
from jax.experimental import pallas as pl
from jax.experimental.pallas import tpu as pltpu

F32 = jnp.float32
BF16 = jnp.bfloat16

D = 2048
S = 8192
CTX = 256
TT = S + CTX
GRID_W = 64
HEADS = 8
Q_LORA = 512
KV_LORA = 512
NOPE = 128
ROPE = 64
VH = 128
HQ = 256
THETA = 10000.0
QSCALE = (NOPE + ROPE) ** -0.5 * float(np.log2(np.e))
FW = 1024
FG = 128
N_E = 32
N_G = 8
EPG = 4
D_E = 512
EPS = 1e-6
NMOD = 6

TM = 256
TQ = 256
MOE_TM = 128
P_ROWS = 2 * S + N_E * MOE_TM
MOE_NT = P_ROWS // MOE_TM
FFT_N1 = 128
FFT_N2 = 64

VMEM_LIMIT = 56 * 1024 * 1024


def _cp(sem, limit=VMEM_LIMIT):
    return pltpu.CompilerParams(dimension_semantics=sem, vmem_limit_bytes=limit)


def _const_spec(shape):
    n = len(shape)
    return pl.BlockSpec(shape, lambda *a: (0,) * n, pipeline_mode=pl.Buffered(1))


SLABS = D // 128


def _load_rows(ref, n):
    return jnp.concatenate([ref[pl.ds(j, n, stride=SLABS), :] for j in range(SLABS)], axis=1)


def _store_rows(ref, val, n):
    for j in range(SLABS):
        ref[pl.ds(j, n, stride=SLABS), :] = val[:, j * 128:(j + 1) * 128]


def _rms(x):
    return x * lax.rsqrt(jnp.mean(x * x, axis=-1, keepdims=True) + EPS)


def _silu(x):
    return x * jax.nn.sigmoid(x)


ADA_BN = 1024


def _ada_kernel(cb_ref, w_ref, b_ref, o_ref):
    a0 = _silu(cb_ref[0])
    a1 = _silu(cb_ref[1])
    rows = []
    for j in range(ADA_BN // 128):
        w = w_ref[0, :, j * 128:(j + 1) * 128]
        r0 = jnp.sum(w * a0, axis=0, keepdims=True)
        r1 = jnp.sum(w * a1, axis=0, keepdims=True)
        rows.append(jnp.concatenate([r0, r1, jnp.zeros((6, 128), F32)], axis=0))
    o_ref[0] = jnp.concatenate(rows, axis=1) + b_ref[0]


def _ada(c, c_ctx, ada_w, ada_b):
    depth = ada_w.shape[0]
    n = ada_w.shape[2]
    cb = jnp.stack([jnp.broadcast_to(c.reshape(D, 1), (D, 128)),
                    jnp.broadcast_to(c_ctx.reshape(D, 1), (D, 128))])
    return pl.pallas_call(
        _ada_kernel,
        grid=(depth, n // ADA_BN),
        in_specs=[pl.BlockSpec((2, D, 128), lambda l, j: (0, 0, 0)),
                  pl.BlockSpec((1, D, ADA_BN), lambda l, j: (l, 0, j)),
                  pl.BlockSpec((1, 1, ADA_BN), lambda l, j: (l, 0, j))],
        out_specs=pl.BlockSpec((1, 8, ADA_BN), lambda l, j: (l, 0, j)),
        out_shape=jax.ShapeDtypeStruct((depth, 8, n), F32),
        compiler_params=_cp(("arbitrary", "arbitrary")),
        name="ada",
    )(cb, ada_w, ada_b.reshape(depth, 1, n))


W_IN_COLS = Q_LORA + KV_LORA + 128 + FW


def _rope_chunk(x, cos, sin):
    lane = lax.broadcasted_iota(jnp.int32, x.shape, 1)
    sw = jnp.where((lane % 32) < 16, pltpu.roll(x, 112, 1), pltpu.roll(x, 16, 1))
    return x * cos + sw * sin


def _inproj_kernel(x_ref, mod_ref, g_ref, win_ref, qg_ref, wq_ref, kg_ref, wk_ref, wv_ref,
                   cos_ref, sin_ref, dft_ref, q_out, k_out, v_out, z_out):
    i = pl.program_id(0)
    is_ctx = i >= S // TM
    mod = jnp.where(is_ctx, mod_ref[0, 1:2, :], mod_ref[0, 0:1, :])
    h = _rms(x_ref[...]) * g_ref[...] * (1.0 + mod[:, D:2 * D]) + mod[:, 0:D]
    u = jnp.dot(h.astype(BF16), win_ref[...], preferred_element_type=F32)
    cos = cos_ref[...]
    sin = sin_ref[...]
    qc = (_rms(u[:, 0:Q_LORA]) * qg_ref[...]).astype(BF16)
    q = jnp.dot(qc, wq_ref[...], preferred_element_type=F32)
    for hd in range(HEADS):
        q_out[:, hd * HQ:hd * HQ + NOPE] = (q[:, hd * HQ:hd * HQ + NOPE] * QSCALE).astype(BF16)
        qr = _rope_chunk(q[:, hd * HQ + NOPE:(hd + 1) * HQ], cos, sin) * QSCALE
        q_out[:, hd * HQ + NOPE:(hd + 1) * HQ] = qr.astype(BF16)
    kc = (_rms(u[:, Q_LORA:Q_LORA + KV_LORA]) * kg_ref[...]).astype(BF16)
    kn = jnp.dot(kc, wk_ref[...], preferred_element_type=F32)
    v = jnp.dot(kc, wv_ref[...], preferred_element_type=F32)
    v_out[...] = jnp.transpose(v).astype(BF16)
    kr = _rope_chunk(u[:, 2 * Q_LORA:2 * Q_LORA + 128], cos, sin).astype(BF16)
    for hd in range(HEADS):
        k_out[:, hd * HQ:hd * HQ + NOPE] = kn[:, hd * NOPE:(hd + 1) * NOPE].astype(BF16)
        k_out[:, hd * HQ + NOPE:(hd + 1) * HQ] = kr
    f0 = 2 * Q_LORA + 128
    for g in range(FW // FG):
        fg = u[:, f0 + g * FG:f0 + (g + 1) * FG].astype(BF16)
        zz = jnp.dot(fg, dft_ref[...], preferred_element_type=F32)
        z_out[0, :, g * FG:(g + 1) * FG] = zz[:, 0:FG].astype(BF16)
        z_out[1, :, g * FG:(g + 1) * FG] = zz[:, FG:2 * FG].astype(BF16)


def _inproj(xc, mods, g1, w_in, qg, wq, kg, wk, wv, cos_t, sin_t, dft):
    nt = TT // TM
    row = lambda i: (i, 0)
    return pl.pallas_call(
        _inproj_kernel,
        grid=(nt,),
        in_specs=[pl.BlockSpec((TM, D), row),
                  pl.BlockSpec((1, 8, 2 * D), lambda i: (0, 0, 0)),
                  _const_spec((1, D)),
                  _const_spec((D, W_IN_COLS)),
                  _const_spec((1, Q_LORA)),
                  _const_spec((Q_LORA, HEADS * HQ)),
                  _const_spec((1, KV_LORA)),
                  _const_spec((KV_LORA, HEADS * NOPE)),
                  _const_spec((KV_LORA, HEADS * VH)),
                  pl.BlockSpec((TM, 128), row),
                  pl.BlockSpec((TM, 128), row),
                  _const_spec((FG, 2 * FG))],
        out_specs=[pl.BlockSpec((TM, HEADS * HQ), row),
                   pl.BlockSpec((TM, HEADS * HQ), row),
                   pl.BlockSpec((HEADS * VH, TM), lambda i: (0, i)),
                   pl.BlockSpec((2, TM, FW), lambda i: (0, i, 0))],
        out_shape=[jax.ShapeDtypeStruct((TT, HEADS * HQ), BF16),
                   jax.ShapeDtypeStruct((TT, HEADS * HQ), BF16),
                   jax.ShapeDtypeStruct((HEADS * VH, TT), BF16),
                   jax.ShapeDtypeStruct((2, TT, FW), BF16)],
        compiler_params=_cp(("arbitrary",)),
        name="inproj",
    )(xc, mods, g1, w_in, qg, wq, kg, wk, wv, cos_t, sin_t, dft)


ATT_CK = 2816


def _attn_kernel(q_ref, k_ref, vt_ref, o_ref):
    q = q_ref[...]
    m = jnp.full((1, TQ), -jnp.inf, F32)
    l = jnp.zeros((1, TQ), F32)
    acc = jnp.zeros((VH, TQ), F32)
    nck = TT // ATT_CK

    def scores(c):
        return lax.dot_general(k_ref[c * ATT_CK:(c + 1) * ATT_CK, :], q, (((1,), (1,)), ((), ())),
                               preferred_element_type=F32)

    s_next = scores(0)
    for c in range(nck):
        ks = slice(c * ATT_CK, (c + 1) * ATT_CK)
        s = s_next
        if c + 1 < nck:
            s_next = scores(c + 1)
        m_new = jnp.maximum(m, jnp.max(s, axis=0, keepdims=True))
        alpha = jnp.exp2(m - m_new)
        p = jnp.exp2(s - m_new)
        l = l * alpha + jnp.sum(p, axis=0, keepdims=True)
        acc = acc * alpha + jnp.dot(vt_ref[:, ks], p.astype(BF16), preferred_element_type=F32)
        m = m_new
    o_ref[...] = jnp.transpose(acc / l).astype(BF16)


def _attention(qp, kp, vt):
    return pl.pallas_call(
        _attn_kernel,
        grid=(HEADS, S // TQ),
        in_specs=[pl.BlockSpec((TQ, HQ), lambda h, i: (i, h)),
                  pl.BlockSpec((TT, HQ), lambda h, i: (0, h)),
                  pl.BlockSpec((VH, TT), lambda h, i: (h, 0))],
        out_specs=pl.BlockSpec((TQ, VH), lambda h, i: (i, h)),
        out_shape=jax.ShapeDtypeStruct((S, HEADS * VH), BF16),
        compiler_params=_cp(("arbitrary", "arbitrary")),
        name="attention",
    )(qp, kp, vt)


FFT1_BN = 2048


def _fft1_kernel(m_ref, z_ref, y_ref):
    y_ref[...] = jnp.dot(m_ref[...], z_ref[...], preferred_element_type=F32).astype(BF16)


FFT2_KB = 8


def _fft2_kernel(g_ref, y_ref, o_ref):
    for j in range(FFT2_KB):
        y = jnp.concatenate([y_ref[0, j], y_ref[1, j]], axis=0)
        o_ref[j] = jnp.dot(g_ref[j], y, preferred_element_type=F32).astype(BF16)


def _fft_tables():
    k1 = np.arange(FFT_N1, dtype=np.float64)
    a = 2.0 * np.pi * np.outer(k1, k1) / FFT_N1
    c1, s1 = np.cos(a), np.sin(a)
    m1 = np.block([[c1, s1], [-s1, c1]])
    t2 = np.arange(FFT_N2, dtype=np.float64)
    k2 = np.arange(FFT_N2, dtype=np.float64)
    th = 2.0 * np.pi * (k1[:, None, None] * t2[None, None, :] / (FFT_N1 * FFT_N2)
                        + k2[None, :, None] * t2[None, None, :] / FFT_N2)
    g = np.concatenate([np.cos(th), np.sin(th)], axis=2) / np.sqrt(float(S * FG))
    cc = np.arange(FG, dtype=np.float64)
    ac = 2.0 * np.pi * np.outer(cc, cc) / FG
    dft = np.concatenate([np.cos(ac), -np.sin(ac)], axis=1)
    return (jnp.asarray(m1, F32).astype(BF16), jnp.asarray(g, F32).astype(BF16),
            jnp.asarray(dft, F32).astype(BF16))


def _fourier_tokens(z, m1, gtab):
    ncol = FFT_N2 * FW
    z2 = z[:, :S].reshape(2 * FFT_N1, ncol)
    y = pl.pallas_call(
        _fft1_kernel,
        grid=(ncol // FFT1_BN,),
        in_specs=[_const_spec((2 * FFT_N1, 2 * FFT_N1)),
                  pl.BlockSpec((2 * FFT_N1, FFT1_BN), lambda j: (0, j))],
        out_specs=pl.BlockSpec((2 * FFT_N1, FFT1_BN), lambda j: (0, j)),
        out_shape=jax.ShapeDtypeStruct((2 * FFT_N1, ncol), BF16),
        compiler_params=_cp(("arbitrary",)),
        name="fft1",
    )(m1, z2)
    y4 = y.reshape(2, FFT_N1, FFT_N2, FW)
    o = pl.pallas_call(
        _fft2_kernel,
        grid=(FFT_N1 // FFT2_KB,),
        in_specs=[pl.BlockSpec((FFT2_KB, FFT_N2, 2 * FFT_N2), lambda k: (k, 0, 0)),
                  pl.BlockSpec((2, FFT2_KB, FFT_N2, FW), lambda k: (0, k, 0, 0))],
        out_specs=pl.BlockSpec((FFT2_KB, FFT_N2, FW), lambda k: (k, 0, 0)),
        out_shape=jax.ShapeDtypeStruct((FFT_N1, FFT_N2, FW), BF16),
        compiler_params=_cp(("arbitrary",)),
        name="fft2",
    )(gtab, y4)
    return jnp.transpose(o, (1, 0, 2)).reshape(S, FW)


def _route(h2, rw_ref, rb_ref, ids_ref, wts_ref):
    hi = h2.astype(BF16)
    lo = (h2 - hi.astype(F32)).astype(BF16)
    r = jnp.dot(jnp.concatenate([hi, lo], axis=0), rw_ref[...], preferred_element_type=F32)
    logits = (r[0:TM, 0:128] + r[0:TM, 128:256]) + (r[TM:2 * TM, 0:128] + r[TM:2 * TM, 128:256])
    lt = jnp.transpose(logits)[0:N_E, :]
    sc = jax.nn.sigmoid(lt)
    bi = sc + rb_ref[...]
    sj = [sc[j * N_G:(j + 1) * N_G] for j in range(EPG)]
    bj = [bi[j * N_G:(j + 1) * N_G] for j in range(EPG)]
    m01, n01 = jnp.maximum(bj[0], bj[1]), jnp.minimum(bj[0], bj[1])
    m23, n23 = jnp.maximum(bj[2], bj[3]), jnp.minimum(bj[2], bj[3])
    gs = jnp.maximum(m01, m23) + jnp.maximum(jnp.minimum(m01, m23), jnp.maximum(n01, n23))
    gmax = jnp.max(gs, axis=0, keepdims=True)
    gi = lax.broadcasted_iota(jnp.int32, gs.shape, 0)
    best = jnp.min(jnp.where(gs == gmax, gi, N_G), axis=0, keepdims=True)
    sel = gi == best
    b = [jnp.sum(jnp.where(sel, x, 0.0), axis=0, keepdims=True) for x in bj]
    s = [jnp.sum(jnp.where(sel, x, 0.0), axis=0, keepdims=True) for x in sj]
    b1 = jnp.maximum(jnp.maximum(b[0], b[1]), jnp.maximum(b[2], b[3]))
    j1 = jnp.full(b1.shape, EPG, jnp.int32)
    for j in reversed(range(EPG)):
        j1 = jnp.where(b[j] == b1, j, j1)
    c = [jnp.where(j1 == j, -jnp.inf, b[j]) for j in range(EPG)]
    b2 = jnp.maximum(jnp.maximum(c[0], c[1]), jnp.maximum(c[2], c[3]))
    j2 = jnp.full(b1.shape, EPG, jnp.int32)
    for j in reversed(range(EPG)):
        j2 = jnp.where(c[j] == b2, j, j2)
    s1 = sum(jnp.where(j1 == j, s[j], 0.0) for j in range(EPG))
    s2 = sum(jnp.where(j2 == j, s[j], 0.0) for j in range(EPG))
    tot = s1 + s2
    zi = jnp.zeros((6,) + b1.shape[1:], jnp.int32)
    zf = jnp.zeros((6,) + b1.shape[1:], F32)
    ids_ref[...] = jnp.concatenate([best * EPG + j1, best * EPG + j2, zi], axis=0)
    wts_ref[...] = jnp.concatenate([s1 / tot, s2 / tot, zf], axis=0)


def _tail(y, x_ref, mod_ref, g2_ref, rw_ref, rb_ref, x1_ref, h2_ref, ids_ref, wts_ref):
    mod = mod_ref[0, 0:1, :]
    x1 = x_ref[...] + mod[:, 0:D] * y
    x1_ref[...] = x1
    h2 = _rms(x1) * g2_ref[...] * (1.0 + mod[:, 2 * D:3 * D]) + mod[:, D:2 * D]
    _store_rows(h2_ref, h2, TM)
    _route(h2, rw_ref, rb_ref, ids_ref, wts_ref)


def _mix_tail_kernel(a_ref, f_ref, wa_ref, wf_ref, x_ref, mod_ref, g2_ref, rw_ref, rb_ref,
                     x1_ref, h2_ref, ids_ref, wts_ref):
    y = (jnp.dot(a_ref[...], wa_ref[...], preferred_element_type=F32)
         + jnp.dot(f_ref[...], wf_ref[...], preferred_element_type=F32))
    _tail(y, x_ref, mod_ref, g2_ref, rw_ref, rb_ref, x1_ref, h2_ref, ids_ref, wts_ref)


def _tail_specs():
    row = lambda i: (i, 0)
    in_specs = [pl.BlockSpec((TM, D), row),
                pl.BlockSpec((1, 8, 3 * D), lambda i: (0, 0, 0)),
                _const_spec((1, D)),
                _const_spec((D, 256)),
                _const_spec((N_E, 1))]
    out_specs = [pl.BlockSpec((TM, D), row),
                 pl.BlockSpec((TM * SLABS, 128), row),
                 pl.BlockSpec((8, TM), lambda i: (0, i)),
                 pl.BlockSpec((8, TM), lambda i: (0, i))]
    out_shape = [jax.ShapeDtypeStruct((S, D), F32),
                 jax.ShapeDtypeStruct((S * SLABS, 128), F32),
                 jax.ShapeDtypeStruct((8, S), jnp.int32),
                 jax.ShapeDtypeStruct((8, S), F32)]
    return in_specs, out_specs, out_shape


def _mix_tail(attn, four, wa, wf, x, mod, g2, rw, rb):
    row = lambda i: (i, 0)
    tin, tout, tshape = _tail_specs()
    return pl.pallas_call(
        _mix_tail_kernel,
        grid=(S // TM,),
        in_specs=[pl.BlockSpec((TM, HEADS * VH), row),
                  pl.BlockSpec((TM, FW), row),
                  _const_spec((HEADS * VH, D)),
                  _const_spec((FW, D))] + tin,
        out_specs=tout,
        out_shape=tshape,
        compiler_params=_cp(("arbitrary",)),
        name="mix_tail",
    )(attn, four, wa, wf, x, mod, g2, rw, rb)


def _conv_in_kernel(h_ref, w_ref, b_ref, cz_ref):
    u = jnp.dot(h_ref[...], w_ref[...], preferred_element_type=F32)
    b_ref[...] = u[:, 0:D].astype(BF16)
    cz_ref[...] = (u[:, D:2 * D] * u[:, 2 * D:3 * D]).astype(BF16)


def _conv_in(h, w):
    row = lambda i: (i, 0)
    return pl.pallas_call(
        _conv_in_kernel,
        grid=(S // TM,),
        in_specs=[pl.BlockSpec((TM, D), row), _const_spec((D, 3 * D))],
        out_specs=[pl.BlockSpec((TM, D), row), pl.BlockSpec((TM, D), row)],
        out_shape=[jax.ShapeDtypeStruct((S, D), BF16), jax.ShapeDtypeStruct((S, D), BF16)],
        compiler_params=_cp(("arbitrary",)),
        name="conv_in",
    )(h, w)


def _conv_tail_kernel(b_ref, cz_ref, prev_ref, next_ref, cw_ref, wo_ref, x_ref, mod_ref, g2_ref,
                      rw_ref, rb_ref, x1_ref, h2_ref, ids_ref, wts_ref):
    i = pl.program_id(0)
    cz = cz_ref[...].astype(F32)
    rid = lax.broadcasted_iota(jnp.int32, cz.shape, 0)
    prev_row = jnp.where(i > 0, prev_ref[15:16, :].astype(F32), 0.0)
    next_row = jnp.where(i < S // TM - 1, next_ref[0:1, :].astype(F32), 0.0)
    dn = jnp.where(rid == 0, prev_row, pltpu.roll(cz, 1, 0))
    up = jnp.where(rid == TM - 1, next_row, pltpu.roll(cz, TM - 1, 0))
    y = dn * cw_ref[0:1, :] + cz * cw_ref[1:2, :] + up * cw_ref[2:3, :]
    yb = (b_ref[...].astype(F32) * y).astype(BF16)
    y = jnp.dot(yb, wo_ref[...], preferred_element_type=F32)
    _tail(y, x_ref, mod_ref, g2_ref, rw_ref, rb_ref, x1_ref, h2_ref, ids_ref, wts_ref)


def _conv_tail(bg, cz, cw, wo, x, mod, g2, rw, rb):
    row = lambda i: (i, 0)
    tin, tout, tshape = _tail_specs()
    nb = TM // 16
    return pl.pallas_call(
        _conv_tail_kernel,
        grid=(S // TM,),
        in_specs=[pl.BlockSpec((TM, D), row),
                  pl.BlockSpec((TM, D), row),
                  pl.BlockSpec((16, D), lambda i: (jnp.maximum(i * nb - 1, 0), 0)),
                  pl.BlockSpec((16, D), lambda i: (jnp.minimum((i + 1) * nb, S // 16 - 1), 0)),
                  _const_spec((8, D)),
                  _const_spec((D, D))] + tin,
        out_specs=tout,
        out_shape=tshape,
        compiler_params=_cp(("arbitrary",)),
        name="conv_tail",
    )(bg, cz, cz, cz, cw, wo, x, mod, g2, rw, rb)


def _row_slab(ref, r):
    return ref.at[pl.ds(pl.multiple_of(r * SLABS, SLABS), SLABS)]


def _moe_kernel(layer, te_ref, first_ref, nxt_ref, nused_ref, src_ref, nsrc_ref, h_ref, wg_hbm, wu_hbm,
                wd_hbm, o_ref, xbuf, xsems, wg_f, wu_f, wd_f, wsems, wg_s, wu_s, wd_s):
    i = pl.program_id(0)
    slot = i % 2
    nused = nused_ref[0]
    last = pl.num_programs(0) - 1

    def xdma(s_ref, st, r):
        return pltpu.make_async_copy(_row_slab(h_ref, s_ref[0, r]), _row_slab(xbuf.at[st], r), xsems.at[st])

    def xloop(fn):
        def body(r, carry):
            fn(r)
            return carry
        lax.fori_loop(0, MOE_TM, body, 0, unroll=8)

    def wdma(e):
        return [pltpu.make_async_copy(w.at[layer, e], buf, wsems.at[k])
                for k, (w, buf) in enumerate(((wg_hbm, wg_f), (wu_hbm, wu_f), (wd_hbm, wd_f)))]

    @pl.when(i == 0)
    def _():
        for c in wdma(te_ref[0]):
            c.start(priority=1)
        xloop(lambda r: xdma(src_ref, 0, r).start())

    xloop(lambda r: xdma(src_ref, slot, r).wait())

    @pl.when(first_ref[i] == 1)
    def _():
        for c in wdma(te_ref[i]):
            c.wait()
        wg_s[...] = wg_f[...].astype(BF16)
        wu_s[...] = wu_f[...].astype(BF16)
        wd_s[...] = wd_f[...].astype(BF16)

        @pl.when(nxt_ref[i] != te_ref[i])
        def _():
            for c in wdma(nxt_ref[i]):
                c.start(priority=1)

    @pl.when(i < nused)
    def _():
        x = _load_rows(xbuf.at[slot], MOE_TM).astype(BF16)
        for r in range(MOE_TM):
            xdma(nsrc_ref, 1 - slot, r).start()
        g = jnp.dot(x, wg_s[...], preferred_element_type=F32)
        u = jnp.dot(x, wu_s[...], preferred_element_type=F32)
        a = (_silu(g) * u).astype(BF16)
        _store_rows(o_ref, jnp.dot(a, wd_s[...], preferred_element_type=F32), MOE_TM)

    @pl.when(i >= nused)
    def _():
        xloop(lambda r: xdma(nsrc_ref, 1 - slot, r).start())
        o_ref[...] = jnp.zeros(o_ref.shape, F32)

    @pl.when(i == last)
    def _():
        xloop(lambda r: xdma(nsrc_ref, 1 - slot, r).wait())


def _moe(te, first, nxt, nused, src, h2, w_gate, w_up, w_down, layer):
    smem = lambda f: pl.BlockSpec((None, 1, MOE_TM), f, memory_space=pltpu.SMEM)
    hbm = pl.BlockSpec(memory_space=pl.ANY)
    grid_spec = pltpu.PrefetchScalarGridSpec(
        num_scalar_prefetch=4,
        grid=(MOE_NT,),
        in_specs=[smem(lambda i, *_: (i, 0, 0)),
                  smem(lambda i, *_: (jnp.minimum(i + 1, MOE_NT - 1), 0, 0)),
                  hbm, hbm, hbm, hbm],
        out_specs=pl.BlockSpec((MOE_TM * SLABS, 128), lambda i, *_: (i, 0)),
        scratch_shapes=[pltpu.VMEM((2, MOE_TM * SLABS, 128), F32), pltpu.SemaphoreType.DMA((2,)),
                        pltpu.VMEM((D, D_E), F32), pltpu.VMEM((D, D_E), F32), pltpu.VMEM((D_E, D), F32),
                        pltpu.SemaphoreType.DMA((3,)),
                        pltpu.VMEM((D, D_E), BF16), pltpu.VMEM((D, D_E), BF16),
                        pltpu.VMEM((D_E, D), BF16)],
    )
    return pl.pallas_call(
        functools.partial(_moe_kernel, layer),
        grid_spec=grid_spec,
        out_shape=jax.ShapeDtypeStruct((P_ROWS * SLABS, 128), F32),
        compiler_params=_cp(("arbitrary",)),
        name="moe",
    )(te, first, nxt, nused, src, src, h2, w_gate, w_up, w_down)


def _route_plan(ids):
    ef = ids[0:2].T.reshape(-1)
    oh = (ef[:, None] == jnp.arange(N_E, dtype=jnp.int32)[None, :]).astype(jnp.int32)
    nb = 2 * S // 128
    ohb = oh.reshape(nb, 128, N_E).astype(F32)
    tri = jnp.tril(jnp.ones((128, 128), F32))
    inner = jnp.einsum("ij,bjn->bin", tri, ohb, precision=lax.Precision.HIGHEST)
    tot = inner[:, -1, :]
    outer = jnp.cumsum(tot, axis=0) - tot
    csum = (inner + outer[:, None, :]).reshape(2 * S, N_E).astype(jnp.int32)
    rank = jnp.sum(csum * oh, axis=1) - 1
    cnt = csum[-1]
    pc = ((cnt + MOE_TM - 1) // MOE_TM) * MOE_TM
    end = jnp.cumsum(pc)
    off = end - pc
    pos = jnp.sum(oh * off[None, :], axis=1) + rank
    nused = (end[-1] // MOE_TM).astype(jnp.int32)
    ti = jnp.arange(MOE_NT, dtype=jnp.int32)
    te = jnp.sum((ti[:, None] >= (end // MOE_TM)[None, :]).astype(jnp.int32), axis=1)
    te_last = jnp.sum((nused - 1 >= end // MOE_TM).astype(jnp.int32))
    te = jnp.minimum(te, te_last).astype(jnp.int32)
    first = jnp.concatenate([jnp.ones((1,), jnp.int32), (te[1:] != te[:-1]).astype(jnp.int32)])
    ee = jnp.arange(N_E, dtype=jnp.int32)
    later = (cnt > 0)[None, :] & (ee[None, :] > te[:, None])
    nxt = jnp.min(jnp.where(later, ee[None, :], N_E), axis=1)
    nxt = jnp.where(nxt == N_E, te, nxt).astype(jnp.int32)
    pos = pos.astype(jnp.int32)
    tok = jnp.arange(2 * S, dtype=jnp.int32) // 2
    src = jnp.zeros((P_ROWS,), jnp.int32).at[pos].set(tok, unique_indices=True)
    return pos, src, te, first, nxt, nused.reshape(1)


def _moe_block(h2, ids, w_gate, w_up, w_down, layer):
    pos, src, te, first, nxt, nused = _route_plan(ids)
    ys = _moe(te, first, nxt, nused, src.reshape(MOE_NT, 1, MOE_TM), h2, w_gate, w_up, w_down, layer)
    return ys, pos.reshape(S // TM, 1, 2 * TM)


def _combine(pos_ref, npos_ref, ys_ref, x1_ref, w_ref, mod_ref, buf, sems):
    i = pl.program_id(0)
    slot = i % 2

    def dma(p_ref, st, r, k):
        return pltpu.make_async_copy(_row_slab(ys_ref, p_ref[0, 2 * r + k]),
                                     _row_slab(buf.at[st, k], r), sems.at[st])

    def issue(p_ref, st):
        def body(r, carry):
            dma(p_ref, st, r, 0).start()
            dma(p_ref, st, r, 1).start()
            return carry
        lax.fori_loop(0, TM, body, 0, unroll=8)

    @pl.when(i == 0)
    def _():
        issue(pos_ref, 0)

    @pl.when(i + 1 < pl.num_programs(0))
    def _():
        issue(npos_ref, 1 - slot)

    def drain(r, carry):
        dma(pos_ref, slot, r, 0).wait()
        dma(pos_ref, slot, r, 1).wait()
        return carry

    lax.fori_loop(0, TM, drain, 0, unroll=8)
    w = w_ref[...]
    moe = w[:, 0:1] * _load_rows(buf.at[slot, 0], TM) + w[:, 1:2] * _load_rows(buf.at[slot, 1], TM)
    return x1_ref[...] + mod_ref[0, 0:1, :] * moe


def _fuse_next_kernel(pos_ref, npos_ref, ys_ref, x1_ref, w_ref, mod_ref, nmod_ref, g_ref, x2_ref, h_ref,
                      buf, sems):
    x2 = _combine(pos_ref, npos_ref, ys_ref, x1_ref, w_ref, mod_ref, buf, sems)
    x2_ref[...] = x2
    nm = nmod_ref[0, 0:1, :]
    h_ref[...] = (_rms(x2) * g_ref[...] * (1.0 + nm[:, D:2 * D]) + nm[:, 0:D]).astype(BF16)


def _fuse_final_kernel(pos_ref, npos_ref, ys_ref, x1_ref, w_ref, mod_ref, g_ref, o_ref, buf, sems):
    x2 = _combine(pos_ref, npos_ref, ys_ref, x1_ref, w_ref, mod_ref, buf, sems)
    o_ref[...] = _rms(x2) * g_ref[...]


def _fuse_specs():
    row = lambda i: (i, 0)
    last = S // TM - 1
    return [pl.BlockSpec((None, 1, 2 * TM), lambda i: (i, 0, 0), memory_space=pltpu.SMEM),
            pl.BlockSpec((None, 1, 2 * TM), lambda i: (jnp.minimum(i + 1, last), 0, 0),
                         memory_space=pltpu.SMEM),
            pl.BlockSpec(memory_space=pl.ANY),
            pl.BlockSpec((TM, D), row),
            pl.BlockSpec((TM, 8), row),
            pl.BlockSpec((1, 8, D), lambda i: (0, 0, 0))]


def _fuse_scratch():
    return [pltpu.VMEM((2, 2, TM * SLABS, 128), F32), pltpu.SemaphoreType.DMA((2,))]


def _fuse_next(pos, ys, x1, wt, mod_g2, nmod, g):
    row = lambda i: (i, 0)
    return pl.pallas_call(
        _fuse_next_kernel,
        grid=(S // TM,),
        in_specs=_fuse_specs() + [pl.BlockSpec((1, 8, 2 * D), lambda i: (0, 0, 0)),
                                  _const_spec((1, D))],
        out_specs=[pl.BlockSpec((TM, D), row), pl.BlockSpec((TM, D), row)],
        out_shape=[jax.ShapeDtypeStruct((S, D), F32), jax.ShapeDtypeStruct((S, D), BF16)],
        scratch_shapes=_fuse_scratch(),
        compiler_params=_cp(("arbitrary",)),
        name="fuse_next",
    )(pos, pos, ys, x1, wt, mod_g2, nmod, g)


def _fuse_final(pos, ys, x1, wt, mod_g2, g):
    row = lambda i: (i, 0)
    return pl.pallas_call(
        _fuse_final_kernel,
        grid=(S // TM,),
        in_specs=_fuse_specs() + [_const_spec((1, D))],
        out_specs=pl.BlockSpec((TM, D), row),
        out_shape=jax.ShapeDtypeStruct((S, D), F32),
        scratch_shapes=_fuse_scratch(),
        compiler_params=_cp(("arbitrary",)),
        name="fuse_final",
    )(pos, pos, ys, x1, wt, mod_g2, g)


def _rope_tables():
    half = ROPE // 2
    nrow = S // GRID_W
    inv = THETA ** (-jnp.arange(0, half, 2, dtype=F32) / half)
    ang_r = jnp.arange(nrow).astype(F32)[:, None] * inv
    ang_c = jnp.arange(GRID_W).astype(F32)[:, None] * inv
    cr, sr = (jnp.repeat(f(ang_r), GRID_W, axis=0) for f in (jnp.cos, jnp.sin))
    cc, sc = (jnp.tile(f(ang_c), (nrow, 1)) for f in (jnp.cos, jnp.sin))
    one = jnp.ones((S, 64), F32)
    zero = jnp.zeros((S, 64), F32)
    cos_t = jnp.concatenate([cr, cr, cc, cc, one], axis=1)
    sin_t = jnp.concatenate([-sr, sr, -sc, sc, zero], axis=1)
    cos_t = jnp.concatenate([cos_t, jnp.ones((CTX, 128), F32)], axis=0)
    sin_t = jnp.concatenate([sin_t, jnp.zeros((CTX, 128), F32)], axis=0)
    return cos_t, sin_t


def _prep_layer0(attn_in_w, q_up_w, kv_up_w):
    c0, c1, c2 = Q_LORA, Q_LORA + KV_LORA, Q_LORA + KV_LORA + ROPE
    w_in = jnp.concatenate([attn_in_w[:, :c2], jnp.zeros((D, 128 - ROPE), F32), attn_in_w[:, c2:]],
                           axis=1).astype(BF16)
    qw = q_up_w.reshape(Q_LORA, HEADS, NOPE + ROPE)
    wq = jnp.concatenate([qw, jnp.zeros((Q_LORA, HEADS, HQ - NOPE - ROPE), F32)],
                         axis=2).reshape(Q_LORA, HEADS * HQ).astype(BF16)
    kvw = kv_up_w.reshape(KV_LORA, HEADS, NOPE + VH)
    wk = kvw[:, :, :NOPE].reshape(KV_LORA, HEADS * NOPE).astype(BF16)
    wv = kvw[:, :, NOPE:].reshape(KV_LORA, HEADS * VH).astype(BF16)
    del c0, c1
    return w_in, wq, wk, wv


def _prep_router(router_w, router_b):
    perm = np.array([EPG * (s % N_G) + s // N_G for s in range(N_E)])
    rw = jnp.concatenate([router_w[:, perm], jnp.zeros((D, 128 - N_E), F32)], axis=1)
    rw_hi = rw.astype(BF16)
    rw_lo = (rw - rw_hi.astype(F32)).astype(BF16)
    rb = router_b[perm].reshape(N_E, 1)
    return jnp.concatenate([rw_hi, rw_lo], axis=1), rb


def kernel(x, c, ctx, c_ctx, ada_w, ada_b, norm1_g, norm2_g, attn_in_w, q_norm_g, q_up_w, kv_norm_g,
           kv_up_w, conv_in_w, conv_w, mix_out_w, router_w, router_b, moe_w_gate, moe_w_up,
           moe_w_down, final_norm_g):
    x = x.reshape(S, D)
    mods = _ada(c.reshape(D), c_ctx, ada_w, ada_b)
    rw, rb = _prep_router(router_w, router_b)
    m1, gtab, dft = _fft_tables()
    cos_t, sin_t = _rope_tables()

    w_in, wq, wk, wv = _prep_layer0(attn_in_w[0], q_up_w[0], kv_up_w[0])
    xc = jnp.concatenate([x, ctx.reshape(CTX, D)], axis=0)
    qp, kp, vt, z = _inproj(xc, mods[0:1, :, 0:2 * D], norm1_g[0:1], w_in, q_norm_g[0:1], wq,
                           kv_norm_g[0:1], wk, wv, cos_t, sin_t, dft)
    attn = _attention(qp, kp, vt)
    four = _fourier_tokens(z, m1, gtab)
    wo = mix_out_w[0].astype(BF16)
    x1, h2, ids, wts = _mix_tail(attn, four, wo[:HEADS * VH], wo[HEADS * VH:], x,
                                 mods[0:1, :, 2 * D:5 * D], norm2_g[0:1], rw, rb)
    ys, pos = _moe_block(h2, ids, moe_w_gate, moe_w_up, moe_w_down, 0)
    x2, h = _fuse_next(pos, ys, x1, wts.T, mods[0:1, :, 5 * D:6 * D], mods[1:2, :, 0:2 * D],
                       norm1_g[1:2])

    bg, cz = _conv_in(h, conv_in_w[0].astype(BF16))
    cw = jnp.concatenate([conv_w[0], jnp.zeros((5, D), F32)], axis=0)
    x1, h2, ids, wts = _conv_tail(bg, cz, cw, mix_out_w[1].astype(BF16), x2,
                                  mods[1:2, :, 2 * D:5 * D], norm2_g[1:2], rw, rb)
    ys, pos = _moe_block(h2, ids, moe_w_gate, moe_w_up, moe_w_down, 1)
    out = _fuse_final(pos, ys, x1, wts.T, mods[1:2, :, 5 * D:6 * D], final_norm_g.reshape(1, D))
    return out.reshape(1, S, D)
```
